```python
import math
import jax, jax.numpy as jnp
from jax import lax
import numpy as np


D_MODEL = 1024
BATCH = 16
SEQ = 2048
DEPTH = 2
DEC_BATCH = 8
DEC_SEQ = 8192
PAST_LEN = 128

D_PLE = 256
RW_HEADS = 8
RW_HEAD_DIM = 64
RW_WIDTH = RW_HEADS * RW_HEAD_DIM
W_LORA = 64
A_LORA = 64
RW_CONV_COLS = 3 * RW_WIDTH + 2 * W_LORA + A_LORA
DA_HEADS = 4
DA_QK_DIM = 64
DA_V_DIM = 2 * DA_QK_DIM
DA_WIDTH = DA_HEADS * DA_V_DIM
MIX_WIDTH = RW_WIDTH + DA_WIDTH
IN_COLS = RW_CONV_COLS + RW_WIDTH + 4 * DA_WIDTH
CONV_WIDTH = 3
Q_BLOCK = 128
NORM_EPS = 1e-6
LN_X_EPS = 64e-5
KK_EPS = 1e-12
ALIBI_MAX_BIAS = 8.0

kernel_name = 'hybrid_rwkv7_diffattn_encoder'


def rmsnorm(x, g, eps=NORM_EPS):
    xf = x.astype(jnp.float32)
    y = xf * lax.rsqrt(jnp.mean(xf * xf, axis=-1, keepdims=True) + eps)
    return (y * g.astype(jnp.float32)).astype(x.dtype)


def centred_conv(x, w, b):
    pad = CONV_WIDTH // 2
    S = x.shape[1]
    xp = jnp.pad(x, ((0, 0), (pad, pad), (0, 0)))
    out = b
    for i in range(CONV_WIDTH):
        out = out + xp[:, i:i + S] * w[i]
    return out


def both_dirs(t):
    tm = jnp.moveaxis(t, 1, 0)
    return jnp.stack([tm, tm[::-1]], axis=1)


def rwkv7_mixer(c, w0, w_up, a0, a_up, k_k, k_a, r_k, ln_g, ln_b):
    f32 = jnp.float32
    B, S, _ = c.shape
    H, N = RW_HEADS, RW_HEAD_DIM
    o = 3 * RW_WIDTH
    r = c[..., :RW_WIDTH].astype(f32).reshape(B, S, H, N)
    k = c[..., RW_WIDTH:2 * RW_WIDTH].astype(f32).reshape(B, S, H, N)
    v = c[..., 2 * RW_WIDTH:o].astype(f32).reshape(B, S, H, N)
    zw = c[..., o:o + 2 * W_LORA].astype(f32).reshape(B, S, 2, W_LORA)
    za = c[..., o + 2 * W_LORA:].astype(f32)
    w_logit = w0.astype(f32) + jnp.einsum('bsdr,drc->bsdc', jnp.tanh(zw), w_up.astype(f32))
    decay = jnp.exp(-jnp.exp(-jax.nn.softplus(-w_logit) - 0.5)).reshape(B, S, 2, H, N)
    a = jax.nn.sigmoid(a0.astype(f32) + za @ a_up.astype(f32)).reshape(B, S, H, N)
    kk = k * k_k.astype(f32).reshape(H, N)
    kk = kk * lax.rsqrt(jnp.sum(kk * kk, axis=-1, keepdims=True) + KK_EPS)
    k = k * (1.0 + (a - 1.0) * k_a.astype(f32).reshape(H, N))
    dtm = jnp.moveaxis(decay, 1, 0)
    w_s = jnp.stack([dtm[:, :, 0], dtm[::-1, :, 1]], axis=1)
    xs = (both_dirs(r), w_s, both_dirs(k), both_dirs(v), both_dirs(-kk), both_dirs(kk * a))

    def step(state, inp):
        r_t, w_t, k_t, v_t, a_t, b_t = inp
        sa = jnp.einsum('dbhij,dbhj->dbhi', state, a_t)
        state = (state * w_t[..., None, :] + sa[..., :, None] * b_t[..., None, :]
                 + v_t[..., :, None] * k_t[..., None, :])
        y_t = jnp.einsum('dbhij,dbhj->dbhi', state, r_t)
        return state, y_t

    state0 = jnp.zeros((2, B, H, N, N), f32)
    _, ys = lax.scan(step, state0, xs)
    y = jnp.moveaxis(ys[:, 0] + ys[::-1, 1], 0, 1)
    mu = jnp.mean(y, axis=-1, keepdims=True)
    var = jnp.mean(jnp.square(y - mu), axis=-1, keepdims=True)
    y = ((y - mu) * lax.rsqrt(var + LN_X_EPS) * ln_g.astype(f32).reshape(H, N)
         + ln_b.astype(f32).reshape(H, N))
    bonus = jnp.sum(r * k * r_k.astype(f32), axis=-1, keepdims=True) * v
    return (y + bonus).reshape(B, S, RW_WIDTH)


def diff_attention_mixer(q, k, v, q_norm, k_norm, lam_vec, subln, lam_init):
    f32 = jnp.float32
    B, S, _ = q.shape
    H, d, dv = DA_HEADS, DA_QK_DIM, DA_V_DIM
    q = rmsnorm(q.reshape(B, S, H, 2, d), q_norm) * (d ** -0.5)
    k = rmsnorm(k.reshape(B, S, H, 2, d), k_norm)
    v = v.reshape(B, S, H, dv)
    lv = lam_vec.astype(f32)
    lam = jnp.exp(jnp.sum(lv[0] * lv[1])) - jnp.exp(jnp.sum(lv[2] * lv[3])) + lam_init
    slopes = 2.0 ** (-ALIBI_MAX_BIAS * jnp.arange(1, H + 1, dtype=f32) / H)
    nb = S // Q_BLOCK
    q_blocks = jnp.moveaxis(q.reshape(B, nb, Q_BLOCK, H, 2, d), 1, 0)
    starts = jnp.arange(nb, dtype=jnp.int32) * Q_BLOCK
    kpos = jnp.arange(S, dtype=jnp.int32)

    def attend(args):
        qb, q0 = args
        s = jnp.einsum('bqhmd,bkhmd->bhmqk', qb, k, preferred_element_type=f32)
        qpos = q0 + jnp.arange(Q_BLOCK, dtype=jnp.int32)
        dist = jnp.abs(qpos[:, None] - kpos[None, :]).astype(f32)
        s = s - slopes[None, :, None, None, None] * dist
        pr = jax.nn.softmax(s, axis=-1)
        att = pr[:, :, 0] - lam * pr[:, :, 1]
        return jnp.einsum('bhqk,bkhe->bqhe', att.astype(v.dtype), v)

    o = lax.map(attend, (q_blocks, starts))
    o = jnp.moveaxis(o, 0, 1).reshape(B, S, H, dv)
    o = rmsnorm(o, subln) * (1.0 - lam_init)
    return o.reshape(B, S, DA_WIDTH)


def hybrid_layer(h, p_i, lam_init, prm):
    u = rmsnorm(h, prm['norm_pre'])
    proj = u @ prm['w_in']
    o0 = RW_CONV_COLS
    o1 = o0 + RW_WIDTH
    o2 = o1 + DA_WIDTH
    o3 = o2 + DA_WIDTH
    o4 = o3 + DA_WIDTH
    c = centred_conv(proj[..., :o0], prm['rw_conv_w'], prm['rw_conv_b'])
    y_rw = rwkv7_mixer(c, prm['rw_w0'], prm['rw_w_up'], prm['rw_a0'], prm['rw_a_up'],
                       prm['rw_k_k'], prm['rw_k_a'], prm['rw_r_k'], prm['rw_ln_g'], prm['rw_ln_b'])
    y_rw = y_rw.astype(h.dtype) * jax.nn.silu(proj[..., o0:o1])
    y_da = diff_attention_mixer(proj[..., o1:o2], proj[..., o2:o3], proj[..., o3:o4],
                                prm['da_q_norm'], prm['da_k_norm'], prm['da_lambda'],
                                prm['da_subln'], lam_init)
    y_da = y_da.astype(h.dtype) * jax.nn.silu(proj[..., o4:])
    h = h + jnp.concatenate([y_rw, y_da], axis=-1) @ prm['w_out']
    e = rmsnorm(p_i @ prm['ple_proj'], prm['ple_norm'])
    gate = jax.nn.sigmoid(h @ prm['ple_gate_w'] + prm['ple_gate_b'])
    return h + gate * e


def run_trunk(x, p, params):
    h = x
    for i in range(DEPTH):
        prm = {name: arr[i] for name, arr in params.items()}
        lam_init = 0.8 - 0.6 * math.exp(-0.3 * i)
        h = hybrid_layer(h, p[i], lam_init, prm)
    return h


def setup_inputs(seed: int = 0) -> dict:
    key = jax.random.key(seed)
    ks = jax.random.split(key, 32)
    f32 = jnp.float32

    def nrm(k, shape, s):
        return s * jax.random.normal(k, shape, f32)

    return {
        'x_prompt': nrm(ks[0], (BATCH, SEQ, D_MODEL), 1.0),
        'x_sample': nrm(ks[1], (DEC_BATCH, DEC_SEQ, D_MODEL), 1.0),
        'p_prompt': nrm(ks[2], (DEPTH, BATCH, SEQ, D_PLE), 1.0),
        'p_sample': nrm(ks[3], (DEPTH, DEC_BATCH, DEC_SEQ, D_PLE), 1.0),
        'norm_pre': 1.0 + nrm(ks[4], (DEPTH, D_MODEL), 0.05),
        'w_in': nrm(ks[5], (DEPTH, D_MODEL, IN_COLS), D_MODEL ** -0.5),
        'w_out': nrm(ks[6], (DEPTH, MIX_WIDTH, D_MODEL), MIX_WIDTH ** -0.5),
        'rw_conv_w': jnp.array([0.25, 1.0, 0.25], f32)[None, :, None]
                     + nrm(ks[7], (DEPTH, CONV_WIDTH, RW_CONV_COLS), 0.05),
        'rw_conv_b': nrm(ks[8], (DEPTH, RW_CONV_COLS), 0.01),
        'rw_w0': nrm(ks[9], (DEPTH, 2, RW_WIDTH), 1.0) - 0.5,
        'rw_w_up': nrm(ks[10], (DEPTH, 2, W_LORA, RW_WIDTH), 0.5 * W_LORA ** -0.5),
        'rw_a0': nrm(ks[11], (DEPTH, RW_WIDTH), 0.1),
        'rw_a_up': nrm(ks[12], (DEPTH, A_LORA, RW_WIDTH), 0.5 * A_LORA ** -0.5),
        'rw_k_k': 0.85 + nrm(ks[13], (DEPTH, RW_WIDTH), 0.05),
        'rw_k_a': 1.0 + nrm(ks[14], (DEPTH, RW_WIDTH), 0.05),
        'rw_r_k': nrm(ks[15], (DEPTH, RW_HEADS, RW_HEAD_DIM), 0.1),
        'rw_ln_g': 1.0 + nrm(ks[16], (DEPTH, RW_WIDTH), 0.05),
        'rw_ln_b': nrm(ks[17], (DEPTH, RW_WIDTH), 0.01),
        'da_q_norm': 1.0 + nrm(ks[18], (DEPTH, 2, DA_QK_DIM), 0.05),
        'da_k_norm': 1.0 + nrm(ks[19], (DEPTH, 2, DA_QK_DIM), 0.05),
        'da_lambda': nrm(ks[20], (DEPTH, 4, DA_QK_DIM), 0.1),
        'da_subln': 1.0 + nrm(ks[21], (DEPTH, DA_V_DIM), 0.05),
        'ple_proj': nrm(ks[22], (DEPTH, D_PLE, D_MODEL), D_PLE ** -0.5),
        'ple_norm': 1.0 + nrm(ks[23], (DEPTH, D_MODEL), 0.05),
        'ple_gate_w': nrm(ks[24], (DEPTH, D_MODEL, D_MODEL), D_MODEL ** -0.5),
        'ple_gate_b': nrm(ks[25], (DEPTH, D_MODEL), 0.01),
    }


def reference(x_prompt, x_sample, p_prompt, p_sample, norm_pre, w_in, w_out,
              rw_conv_w, rw_conv_b, rw_w0, rw_w_up, rw_a0, rw_a_up, rw_k_k, rw_k_a, rw_r_k,
              rw_ln_g, rw_ln_b, da_q_norm, da_k_norm, da_lambda, da_subln,
              ple_proj, ple_norm, ple_gate_w, ple_gate_b):
    params = dict(norm_pre=norm_pre, w_in=w_in, w_out=w_out,
                  rw_conv_w=rw_conv_w, rw_conv_b=rw_conv_b, rw_w0=rw_w0, rw_w_up=rw_w_up,
                  rw_a0=rw_a0, rw_a_up=rw_a_up, rw_k_k=rw_k_k, rw_k_a=rw_k_a, rw_r_k=rw_r_k,
                  rw_ln_g=rw_ln_g, rw_ln_b=rw_ln_b, da_q_norm=da_q_norm, da_k_norm=da_k_norm,
                  da_lambda=da_lambda, da_subln=da_subln, ple_proj=ple_proj, ple_norm=ple_norm,
                  ple_gate_w=ple_gate_w, ple_gate_b=ple_gate_b)
    y_prompt = run_trunk(x_prompt, p_prompt, params)
    y_sample = run_trunk(x_sample, p_sample, params)
    return (y_prompt, y_sample)
```

```python
import functools
import math

import numpy as np
import jax
import jax.numpy as jnp
from jax import lax
from jax.experimental import pallas as pl
from jax.experimental.pallas import tpu as pltpu

f32 = jnp.float32
bf16 = jnp.bfloat16

D_MODEL = 1024
D_PLE = 256
RW_WIDTH = 512
HEAD_DIM = 64
W_LORA = 64
CONV_COLS = 3 * RW_WIDTH + 2 * W_LORA + 64
CONV_PAD = 1792
DA_HEADS = 4
DA_V_DIM = 128
DA_WIDTH = 512
REST_COLS = 5 * 512
NORM_EPS = 1e-6
LN_X_EPS = 64e-5
KK_EPS = 1e-12
ALIBI_MAX_BIAS = 8.0
DECAY_SCALE = math.exp(-0.5)

CHUNK = 64
GROUP = 256
TOKEN_TILE = 256
VMEM_LIMIT = 48 * 1024 * 1024


def _dot(a, b):
    return jnp.dot(a, b, preferred_element_type=f32)


def _dot_nt(a, b):
    return lax.dot_general(a, b, (((1,), (1,)), ((), ())), preferred_element_type=f32)


def _split2(x):
    hi = x.astype(bf16)
    lo = (x - hi.astype(f32)).astype(bf16)
    return hi, lo


def _split3(x):
    hi = x.astype(bf16)
    r1 = x - hi.astype(f32)
    mid = r1.astype(bf16)
    lo = (r1 - mid.astype(f32)).astype(bf16)
    return hi, mid, lo


def _dot3(a, b, nt=False):
    d = _dot_nt if nt else _dot
    ah, al = _split2(a)
    bh, bl = _split2(b)
    return d(ah, bh) + d(ah, bl) + d(al, bh)


def _dot_exact01(m01, x, parts):
    ps = _split3(x) if parts == 3 else _split2(x)
    acc = _dot(m01, ps[0])
    for p in ps[1:]:
        acc = acc + _dot(m01, p)
    return acc


def _segsum(x, seg):
    hi, lo = _split2(x)
    return _dot(hi, seg) + _dot(lo, seg)


def _sigmoid(x):
    return 1.0 / (1.0 + jnp.exp(-x))


def _silu(x):
    return x * _sigmoid(x)


def _inproj_body(x_ref, g_ref, w_ref, conv_ref, rest_ref):
    x = x_ref[...]
    ms = jnp.mean(x * x, axis=-1, keepdims=True)
    u = (x * lax.rsqrt(ms + NORM_EPS) * g_ref[...]).astype(bf16)
    for c0 in range(0, CONV_PAD, 256):
        conv_ref[:, c0:c0 + 256] = _dot(u, w_ref[:, c0:c0 + 256])
    for c0 in range(0, REST_COLS, 256):
        rest_ref[:, c0:c0 + 256] = _dot(u, w_ref[:, CONV_PAD + c0:CONV_PAD + c0 + 256])


def _inproj(h, norm_g, w_pad):
    T = h.shape[0]
    tm = TOKEN_TILE
    return pl.pallas_call(
        _inproj_body,
        grid=(T // tm,),
        in_specs=[
            pl.BlockSpec((tm, D_MODEL), lambda i: (i, 0)),
            pl.BlockSpec((1, D_MODEL), lambda i: (0, 0)),
            pl.BlockSpec((D_MODEL, CONV_PAD + REST_COLS), lambda i: (0, 0)),
        ],
        out_specs=[
            pl.BlockSpec((tm, CONV_PAD), lambda i: (i, 0)),
            pl.BlockSpec((tm, REST_COLS), lambda i: (i, 0)),
        ],
        out_shape=[
            jax.ShapeDtypeStruct((T, CONV_PAD), f32),
            jax.ShapeDtypeStruct((T, REST_COLS), f32),
        ],
        compiler_params=pltpu.CompilerParams(
            dimension_semantics=("arbitrary",), vmem_limit_bytes=VMEM_LIMIT),
        name="inproj",
    )(h, norm_g, w_pad)


def _rwkv_prep_body(tiles_per_seq, main_ref, prev_ref, next_ref, cw_ref, cb_ref, w0_ref, wup_ref,
                    a0_ref, aup_ref, kk_ref, ka_ref, rk_ref, seg_ref, tri_ref,
                    slab_ref, v_ref, bonus_ref):
    tm = main_ref.shape[0]
    i = pl.program_id(0)
    local = i % tiles_per_seq
    is_first = local == 0
    is_last = local == tiles_per_seq - 1

    xm = main_ref[...]
    row = lax.broadcasted_iota(jnp.int32, (tm, 1), 0)
    prev_row = jnp.where(is_first, 0.0, prev_ref[7:8, :])
    next_row = jnp.where(is_last, 0.0, next_ref[0:1, :])
    xp = jnp.where(row == 0, prev_row, pltpu.roll(xm, 1, 0))
    xn = jnp.where(row == tm - 1, next_row, pltpu.roll(xm, tm - 1, 0))
    cw = cw_ref[...]
    c = cb_ref[...] + xp * cw[0:1] + xm * cw[1:2] + xn * cw[2:3]

    r = c[:, 0:512]
    k = c[:, 512:1024]
    v = c[:, 1024:1536]
    zw = c[:, 1536:1664]
    za = c[:, 1664:1792]

    wl = w0_ref[...] + _dot(jnp.tanh(zw).astype(bf16), wup_ref[...])
    logw = -DECAY_SCALE * _sigmoid(wl)
    a = _sigmoid(a0_ref[...] + _dot(za.astype(bf16), aup_ref[...]))
    seg = seg_ref[...]
    kk = k * kk_ref[...]
    kk = kk * lax.rsqrt(_segsum(kk * kk, seg) + KK_EPS)
    kmod = k * (1.0 + (a - 1.0) * ka_ref[...])
    kka = kk * a
    bonus_ref[...] = _segsum(r * kmod * rk_ref[...], seg) * v
    v_ref[...] = v

    for d in range(2):
        lw = logw[:, d * 512:(d + 1) * 512]
        cums = _dot_exact01(tri_ref[d], lw, 3)
        cum_in = cums[:tm]
        rev_ex = cums[tm:]
        e_pos = jnp.exp(cum_in)
        e_neg = jnp.exp(-cum_in)
        e_rev = jnp.exp(rev_ex)
        slab_ref[d, 0] = r * e_pos
        slab_ref[d, 1] = kmod * e_neg
        slab_ref[d, 2] = -kk * jnp.exp(cum_in - lw)
        slab_ref[d, 3] = kka * e_neg
        slab_ref[d, 4] = kka * e_rev
        slab_ref[d, 5] = kmod * e_rev
        slab_ref[d, 6] = jnp.exp(cum_in + rev_ex)


def _rwkv_prep(conv_in, seq_len, prm):
    T = conv_in.shape[0]
    tm = TOKEN_TILE
    tiles_per_seq = seq_len // tm
    n8 = T // 8
    const = lambda shape: pl.BlockSpec(shape, lambda i: (0,) * len(shape))
    return pl.pallas_call(
        functools.partial(_rwkv_prep_body, tiles_per_seq),
        grid=(T // tm,),
        in_specs=[
            pl.BlockSpec((tm, CONV_PAD), lambda i: (i, 0)),
            pl.BlockSpec((8, CONV_PAD), lambda i: (jnp.maximum(i * (tm // 8) - 1, 0), 0)),
            pl.BlockSpec((8, CONV_PAD), lambda i: (jnp.minimum((i + 1) * (tm // 8), n8 - 1), 0)),
            const((3, CONV_PAD)), const((1, CONV_PAD)), const((1, 1024)), const((128, 1024)),
            const((1, 512)), const((128, 512)), const((1, 512)), const((1, 512)), const((1, 512)),
            const((512, 512)), const((2, 2 * tm, tm)),
        ],
        out_specs=[
            pl.BlockSpec((2, 7, tm, 512), lambda i: (0, 0, i, 0)),
            pl.BlockSpec((tm, 512), lambda i: (i, 0)),
            pl.BlockSpec((tm, 512), lambda i: (i, 0)),
        ],
        out_shape=[
            jax.ShapeDtypeStruct((2, 7, T, 512), f32),
            jax.ShapeDtypeStruct((T, 512), f32),
            jax.ShapeDtypeStruct((T, 512), f32),
        ],
        compiler_params=pltpu.CompilerParams(
            dimension_semantics=("arbitrary",), vmem_limit_bytes=VMEM_LIMIT),
        name="rwkv_prep",
    )(conv_in, conv_in, conv_in, prm["conv_w"], prm["conv_b"], prm["w0"], prm["w_up"],
      prm["a0"], prm["a_up"], prm["k_k"], prm["k_a"], prm["r_k"], prm["seg"], prm["tri"])


def _unit_lower_inverse(m, eye):
    t = eye + m
    q = _dot3(m, m)
    for _ in range(4):
        p = _dot3(jnp.concatenate([t, q], axis=0), q)
        t = t + p[:CHUNK]
        q = p[CHUNK:]
    return t + _dot3(t, q)


def _group_chunk(rt, kt, at, bt, bp, kp, pdiag, v, z, strict2, incl2, head_masks, eye, blockdiag, eye_g):
    L = CHUNK
    x = jnp.concatenate([at, rt], axis=0)
    y = jnp.concatenate([bt, kt], axis=0)
    vpad = jnp.concatenate([jnp.zeros_like(v), v], axis=0)
    m_bots, t_invs = [], []
    w1 = jnp.zeros((L, GROUP), f32)
    for hm in head_masks:
        g = _dot3(x * hm, y, nt=True)
        m_top = g[:L] * strict2
        m_bots.append(g[L:] * incl2)
        t_invs.append(_unit_lower_inverse(m_top[:, :L], eye))
        w1 = w1 + hm * _dot3(m_top, vpad)
    rhs2 = jnp.concatenate([at, w1], axis=1)
    a_hat = jnp.zeros((L, GROUP), f32)
    u_hat = jnp.zeros((L, GROUP), f32)
    for hm, t_inv in zip(head_masks, t_invs):
        au = _dot3(t_inv, rhs2)
        a_hat = a_hat + hm * au[:, :GROUP]
        u_hat = u_hat + hm * au[:, GROUP:]
    rhs3 = jnp.concatenate(
        [jnp.concatenate([a_hat, u_hat], axis=1),
         jnp.concatenate([jnp.zeros_like(v), v], axis=1)], axis=0)
    r_hat = rt
    y_hat = jnp.zeros((L, GROUP), f32)
    for hm, m_bot in zip(head_masks, m_bots):
        o = _dot3(m_bot, rhs3)
        r_hat = r_hat + hm * o[:, :GROUP]
        y_hat = y_hat + hm * o[:, GROUP:]
    bp_t = bp.T
    kp_t = kp.T
    phi_t = _dot3(bp_t, a_hat) * blockdiag + jnp.where(eye_g, pdiag, 0.0)
    psi_t = _dot3(jnp.concatenate([bp_t, kp_t], axis=1), jnp.concatenate([u_hat, v], axis=0)) * blockdiag
    y_out = _dot3(r_hat, z) + y_hat
    z_next = _dot3(phi_t, z) + psi_t
    return y_out, z_next


def _rwkv_scan_body(slab_f_ref, slab_b_ref, v_f_ref, v_b_ref, y_f_ref, y_b_ref, z_ref):
    L = CHUNK

    @pl.when(pl.program_id(1) == 0)
    def _():
        z_ref[...] = jnp.zeros_like(z_ref)

    ri = lax.broadcasted_iota(jnp.int32, (L, 2 * L), 0)
    ci = lax.broadcasted_iota(jnp.int32, (L, 2 * L), 1) % L
    rl = lax.broadcasted_iota(jnp.int32, (L, L), 0)
    cl = lax.broadcasted_iota(jnp.int32, (L, L), 1)
    eye = (rl == cl).astype(f32)
    rg = lax.broadcasted_iota(jnp.int32, (GROUP, GROUP), 0)
    cg = lax.broadcasted_iota(jnp.int32, (GROUP, GROUP), 1)
    blockdiag = ((rg // HEAD_DIM) == (cg // HEAD_DIM)).astype(f32)
    eye_g = rg == cg
    lane = lax.broadcasted_iota(jnp.int32, (1, GROUP), 1) // HEAD_DIM
    head_masks = [(lane == h).astype(f32) for h in range(GROUP // HEAD_DIM)]

    for d, (slab_ref, v_ref, y_ref) in enumerate(
            ((slab_f_ref, v_f_ref, y_f_ref), (slab_b_ref, v_b_ref, y_b_ref))):
        if d == 0:
            strict2 = (ci < ri).astype(f32)
            incl2 = (ci <= ri).astype(f32)
        else:
            strict2 = (ci > ri).astype(f32)
            incl2 = (ci >= ri).astype(f32)
        for g in range(RW_WIDTH // GROUP):
            lanes = slice(g * GROUP, (g + 1) * GROUP)
            y_out, z_next = _group_chunk(
                slab_ref[0, :, lanes], slab_ref[1, :, lanes], slab_ref[2, :, lanes],
                slab_ref[3, :, lanes], slab_ref[4, :, lanes], slab_ref[5, :, lanes],
                slab_ref[6, 0:1, lanes], v_ref[:, lanes], z_ref[d, g],
                strict2, incl2, head_masks, eye, blockdiag, eye_g)
            y_ref[:, lanes] = y_out
            z_ref[d, g] = z_next


def _rwkv_scan(slab, v, batch, seq_len):
    T = v.shape[0]
    L = CHUNK
    nc = seq_len // L
    fwd = lambda b, c: b * nc + c
    bwd = lambda b, c: b * nc + (nc - 1 - c)
    return pl.pallas_call(
        _rwkv_scan_body,
        grid=(batch, nc),
        in_specs=[
            pl.BlockSpec((None, 7, L, 512), lambda b, c: (0, 0, fwd(b, c), 0)),
            pl.BlockSpec((None, 7, L, 512), lambda b, c: (1, 0, bwd(b, c), 0)),
            pl.BlockSpec((L, 512), lambda b, c: (fwd(b, c), 0)),
            pl.BlockSpec((L, 512), lambda b, c: (bwd(b, c), 0)),
        ],
        out_specs=[
            pl.BlockSpec((L, 512), lambda b, c: (fwd(b, c), 0)),
            pl.BlockSpec((L, 512), lambda b, c: (bwd(b, c), 0)),
        ],
        out_shape=[jax.ShapeDtypeStruct((T, 512), f32), jax.ShapeDtypeStruct((T, 512), f32)],
        scratch_shapes=[pltpu.VMEM((2, RW_WIDTH // GROUP, GROUP, GROUP), f32)],
        compiler_params=pltpu.CompilerParams(
            dimension_semantics=("arbitrary", "arbitrary"), vmem_limit_bytes=VMEM_LIMIT),
        name="rwkv_scan",
    )(slab, slab, v, v)


def _attn_prep_body(rest_ref, qn_ref, kn_ref, seg_ref, q0_ref, q1_ref, k_ref, v_ref):
    seg = seg_ref[...]
    q = rest_ref[:, 512:1024]
    k = rest_ref[:, 1024:1536]
    inv_d = 1.0 / HEAD_DIM
    qms = _segsum(q * q, seg) * inv_d
    kms = _segsum(k * k, seg) * inv_d
    qn = q * lax.rsqrt(qms + NORM_EPS) * qn_ref[...] * (HEAD_DIM ** -0.5)
    kn = k * lax.rsqrt(kms + NORM_EPS) * kn_ref[...]
    lane = lax.broadcasted_iota(jnp.int32, (1, DA_WIDTH), 1)
    first_map = (lane // HEAD_DIM) % 2 == 0
    q0_ref[...] = jnp.where(first_map, qn, 0.0).astype(bf16)
    q1_ref[...] = jnp.where(first_map, 0.0, qn).astype(bf16)
    k_ref[...] = kn.astype(bf16)
    v_ref[...] = rest_ref[:, 1536:2048].astype(bf16)


def _attn_prep(rest, prm):
    T = rest.shape[0]
    tm = TOKEN_TILE
    out = jax.ShapeDtypeStruct((T, DA_WIDTH), bf16)
    spec = pl.BlockSpec((tm, DA_WIDTH), lambda i: (i, 0))
    return pl.pallas_call(
        _attn_prep_body,
        grid=(T // tm,),
        in_specs=[
            pl.BlockSpec((tm, REST_COLS), lambda i: (i, 0)),
            pl.BlockSpec((1, DA_WIDTH), lambda i: (0, 0)),
            pl.BlockSpec((1, DA_WIDTH), lambda i: (0, 0)),
            pl.BlockSpec((512, 512), lambda i: (0, 0)),
        ],
        out_specs=[spec, spec, spec, spec],
        out_shape=[out, out, out, out],
        compiler_params=pltpu.CompilerParams(
            dimension_semantics=("arbitrary",), vmem_limit_bytes=VMEM_LIMIT),
        name="attn_prep",
    )(rest, prm["q_norm"], prm["k_norm"], prm["seg"])


def _flash_body(lam_init, tq, tk, q0_ref, q1_ref, k_ref, v_ref, lam_ref, subln_ref, o_ref,
                m_ref, l_ref, acc_ref):
    h = pl.program_id(1)
    qi = pl.program_id(2)
    ki = pl.program_id(3)

    @pl.when(ki == 0)
    def _():
        m_ref[...] = jnp.full_like(m_ref, -jnp.inf)
        l_ref[...] = jnp.zeros_like(l_ref)
        acc_ref[...] = jnp.zeros_like(acc_ref)

    slope = jnp.exp2(jnp.full((1, 1), -(ALIBI_MAX_BIAS / DA_HEADS), f32) * (h + 1).astype(f32))
    rel = (lax.broadcasted_iota(jnp.int32, (tq, tk), 0) - lax.broadcasted_iota(jnp.int32, (tq, tk), 1)
           + (qi * tq - ki * tk))
    bias = slope * jnp.abs(rel).astype(f32)
    k = k_ref[...]
    v = v_ref[...]
    for mi, q_ref in enumerate((q0_ref, q1_ref)):
        s = _dot_nt(q_ref[...], k) - bias
        m_prev = m_ref[mi]
        m_new = jnp.maximum(m_prev, jnp.max(s, axis=-1, keepdims=True))
        alpha = jnp.exp(m_prev - m_new)
        p = jnp.exp(s - m_new)
        l_ref[mi] = alpha * l_ref[mi] + jnp.sum(p, axis=-1, keepdims=True)
        acc_ref[mi] = alpha * acc_ref[mi] + _dot(p.astype(bf16), v)
        m_ref[mi] = m_new

    @pl.when(ki == pl.num_programs(3) - 1)
    def _():
        lv = lam_ref[...]
        lam = (jnp.exp(jnp.sum(lv[0:1] * lv[1:2], axis=-1, keepdims=True))
               - jnp.exp(jnp.sum(lv[2:3] * lv[3:4], axis=-1, keepdims=True)) + lam_init)
        o = acc_ref[0] / l_ref[0] - lam * (acc_ref[1] / l_ref[1])
        ms = jnp.mean(o * o, axis=-1, keepdims=True)
        o_ref[...] = o * lax.rsqrt(ms + NORM_EPS) * subln_ref[...] * (1.0 - lam_init)


def _flash(q0, q1, k, v, lam_vec, subln, lam_init, batch, seq_len):
    T = q0.shape[0]
    tq = min(256, seq_len)
    tk = min(512, seq_len)
    nq = seq_len // tq
    nk = seq_len // tk
    qspec = pl.BlockSpec((tq, DA_V_DIM), lambda b, h, i, j: (b * nq + i, h))
    kspec = pl.BlockSpec((tk, DA_V_DIM), lambda b, h, i, j: (b * nk + j, h))
    return pl.pallas_call(
        functools.partial(_flash_body, lam_init, tq, tk),
        grid=(batch, DA_HEADS, nq, nk),
        in_specs=[qspec, qspec, kspec, kspec,
                  pl.BlockSpec((4, HEAD_DIM), lambda b, h, i, j: (0, 0)),
                  pl.BlockSpec((1, DA_V_DIM), lambda b, h, i, j: (0, 0))],
        out_specs=pl.BlockSpec((tq, DA_V_DIM), lambda b, h, i, j: (b * nq + i, h)),
        out_shape=jax.ShapeDtypeStruct((T, DA_WIDTH), f32),
        scratch_shapes=[pltpu.VMEM((2, tq, 1), f32), pltpu.VMEM((2, tq, 1), f32),
                        pltpu.VMEM((2, tq, DA_V_DIM), f32)],
        compiler_params=pltpu.CompilerParams(
            dimension_semantics=("arbitrary", "arbitrary", "arbitrary", "arbitrary"),
            vmem_limit_bytes=VMEM_LIMIT),
        name="flash_diff_attn",
    )(q0, q1, k, v, lam_vec, subln)


def _out_body(h_ref, yf_ref, yb_ref, bonus_ref, rest_ref, o_ref, p_ref, seg_ref, lng_ref, lnb_ref,
              wout_ref, pproj_ref, pnorm_ref, gw_ref, gb_ref, out_ref):
    seg = seg_ref[...]
    inv_n = 1.0 / HEAD_DIM
    y = yf_ref[...] + yb_ref[...]
    mu = _segsum(y, seg) * inv_n
    yc = y - mu
    var = _segsum(yc * yc, seg) * inv_n
    y_rw = (yc * lax.rsqrt(var + LN_X_EPS) * lng_ref[...] + lnb_ref[...] + bonus_ref[...])
    y_rw = y_rw * _silu(rest_ref[:, 0:512])
    y_da = o_ref[...] * _silu(rest_ref[:, 2048:2560])
    h1 = (h_ref[...] + _dot(y_rw.astype(bf16), wout_ref[0:512, :])
          + _dot(y_da.astype(bf16), wout_ref[512:1024, :]))
    e = _dot(p_ref[...].astype(bf16), pproj_ref[...])
    e = e * lax.rsqrt(jnp.mean(e * e, axis=-1, keepdims=True) + NORM_EPS) * pnorm_ref[...]
    gate = _sigmoid(_dot(h1.astype(bf16), gw_ref[...]) + gb_ref[...])
    out_ref[...] = h1 + gate * e


def _out_stage(h, y_f, y_b, bonus, rest, o, p, prm):
    T = h.shape[0]
    tm = TOKEN_TILE
    row = lambda w: pl.BlockSpec((tm, w), lambda i: (i, 0))
    const = lambda shape: pl.BlockSpec(shape, lambda i: (0,) * len(shape))
    return pl.pallas_call(
        _out_body,
        grid=(T // tm,),
        in_specs=[row(D_MODEL), row(512), row(512), row(512), row(REST_COLS), row(512), row(D_PLE),
                  const((512, 512)), const((1, 512)), const((1, 512)),
                  const((D_MODEL, D_MODEL)), const((D_PLE, D_MODEL)), const((1, D_MODEL)),
                  const((D_MODEL, D_MODEL)), const((1, D_MODEL))],
        out_specs=row(D_MODEL),
        out_shape=jax.ShapeDtypeStruct((T, D_MODEL), f32),
        compiler_params=pltpu.CompilerParams(
            dimension_semantics=("arbitrary",), vmem_limit_bytes=VMEM_LIMIT),
        name="out_stage",
    )(h, y_f, y_b, bonus, rest, o, p, prm["seg"], prm["ln_g"], prm["ln_b"], prm["w_out"],
      prm["ple_proj"], prm["ple_norm"], prm["gate_w"], prm["gate_b"])


def _chunk_triangles(tm):
    t = np.arange(tm)
    same = (t[:, None] // CHUNK) == (t[None, :] // CHUNK)
    low_incl = same & (t[None, :] <= t[:, None])
    up_strict = same & (t[None, :] > t[:, None])
    up_incl = same & (t[None, :] >= t[:, None])
    low_strict = same & (t[None, :] < t[:, None])
    tri = np.stack([np.concatenate([low_incl, up_strict], 0), np.concatenate([up_incl, low_strict], 0)])
    return jnp.asarray(tri.astype(np.float32), dtype=bf16)


def _layer_params(i, norm_pre, w_in, w_out, rw_conv_w, rw_conv_b, rw_w0, rw_w_up, rw_a0, rw_a_up, rw_k_k,
                  rw_k_a, rw_r_k, rw_ln_g, rw_ln_b, da_q_norm, da_k_norm, da_lambda, da_subln, ple_proj,
                  ple_norm, ple_gate_w, ple_gate_b):
    pad = CONV_PAD - (CONV_COLS)
    w = w_in[i]
    w_pad = jnp.concatenate(
        [w[:, :CONV_COLS], jnp.zeros((D_MODEL, pad), f32), w[:, CONV_COLS:]], axis=1).astype(bf16)
    wup = rw_w_up[i]
    zeros = jnp.zeros((W_LORA, RW_WIDTH), f32)
    w_up_bd = jnp.concatenate(
        [jnp.concatenate([wup[0], zeros], 1), jnp.concatenate([zeros, wup[1]], 1)], 0).astype(bf16)
    a_up_pad = jnp.concatenate([rw_a_up[i], jnp.zeros((64, RW_WIDTH), f32)], 0).astype(bf16)
    lane = np.arange(512)
    seg = jnp.asarray((lane[:, None] // HEAD_DIM == lane[None, :] // HEAD_DIM).astype(np.float32), dtype=bf16)
    return dict(
        norm_pre=norm_pre[i].reshape(1, D_MODEL),
        w_in=w_pad,
        conv_w=jnp.pad(rw_conv_w[i], ((0, 0), (0, pad))),
        conv_b=jnp.pad(rw_conv_b[i], (0, pad)).reshape(1, CONV_PAD),
        w0=rw_w0[i].reshape(1, 2 * RW_WIDTH),
        w_up=w_up_bd,
        a0=rw_a0[i].reshape(1, RW_WIDTH),
        a_up=a_up_pad,
        k_k=rw_k_k[i].reshape(1, RW_WIDTH),
        k_a=rw_k_a[i].reshape(1, RW_WIDTH),
        r_k=rw_r_k[i].reshape(1, RW_WIDTH),
        ln_g=rw_ln_g[i].reshape(1, RW_WIDTH),
        ln_b=rw_ln_b[i].reshape(1, RW_WIDTH),
        q_norm=jnp.tile(da_q_norm[i].reshape(1, 2 * HEAD_DIM), (1, DA_HEADS)),
        k_norm=jnp.tile(da_k_norm[i].reshape(1, 2 * HEAD_DIM), (1, DA_HEADS)),
        lam_vec=da_lambda[i],
        subln=da_subln[i].reshape(1, DA_V_DIM),
        w_out=w_out[i].astype(bf16),
        ple_proj=ple_proj[i].astype(bf16),
        ple_norm=ple_norm[i].reshape(1, D_MODEL),
        gate_w=ple_gate_w[i].astype(bf16),
        gate_b=ple_gate_b[i].reshape(1, D_MODEL),
        seg=seg,
        tri=_chunk_triangles(TOKEN_TILE),
    )


def _layer(h, p, lam_init, prm, batch, seq_len):
    conv_in, rest = _inproj(h, prm["norm_pre"], prm["w_in"])
    slab, v_rw, bonus = _rwkv_prep(conv_in, seq_len, prm)
    y_f, y_b = _rwkv_scan(slab, v_rw, batch, seq_len)
    q0, q1, k, v = _attn_prep(rest, prm)
    o = _flash(q0, q1, k, v, prm["lam_vec"], prm["subln"], lam_init, batch, seq_len)
    return _out_stage(h, y_f, y_b, bonus, rest, o, p, prm)


def _trunk(x, p, layers):
    batch, seq_len, _ = x.shape
    h = x.reshape(batch * seq_len, D_MODEL)
    for i, prm in enumerate(layers):
        lam_init = 0.8 - 0.6 * math.exp(-0.3 * i)
        h = _layer(h, p[i].reshape(batch * seq_len, D_PLE), lam_init, prm, batch, seq_len)
    return h.reshape(batch, seq_len, D_MODEL)


def kernel(x_prompt, x_sample, p_prompt, p_sample, norm_pre, w_in, w_out, rw_conv_w, rw_conv_b, rw_w0, rw_w_up, rw_a0, rw_a_up, rw_k_k, rw_k_a, rw_r_k, rw_ln_g, rw_ln_b, da_q_norm, da_k_norm, da_lambda, da_subln, ple_proj, ple_norm, ple_gate_w, ple_gate_b):
    depth = norm_pre.shape[0]
    layers = [_layer_params(i, norm_pre, w_in, w_out, rw_conv_w, rw_conv_b, rw_w0, rw_w_up, rw_a0, rw_a_up,
                            rw_k_k, rw_k_a, rw_r_k, rw_ln_g, rw_ln_b, da_q_norm, da_k_norm, da_lambda,
                            da_subln, ple_proj, ple_norm, ple_gate_w, ple_gate_b) for i in range(depth)]
    return (_trunk(x_prompt, p_prompt, layers), _trunk(x_sample, p_sample, layers))
```

```python
import functools
import math

import numpy as np
import jax
import jax.numpy as jnp
from jax import lax
from jax.experimental import pallas as pl
from jax.experimental.pallas import tpu as pltpu

f32 = jnp.float32
bf16 = jnp.bfloat16

D_MODEL = 1024
D_PLE = 256
RW_WIDTH = 512
HEAD_DIM = 64
W_LORA = 64
CONV_COLS = 3 * RW_WIDTH + 2 * W_LORA + 64
CONV_PAD = 1792
DA_HEADS = 4
DA_V_DIM = 128
DA_WIDTH = 512
DA_EXT = 256
REST_COLS = 5 * 512
NORM_EPS = 1e-6
LN_X_EPS = 64e-5
KK_EPS = 1e-12
ALIBI_MAX_BIAS = 8.0
DECAY_SCALE = math.exp(-0.5)
POS_SPLIT = 128

CHUNK = 64
GROUP = 256
HEADS_PER_GROUP = GROUP // HEAD_DIM
STACK = HEADS_PER_GROUP * CHUNK
TOKEN_TILE = 256
FLASH_TQ = 256
FLASH_TK = 512
VMEM_LIMIT = 48 * 1024 * 1024


def _dot(a, b):
    return jnp.dot(a, b, preferred_element_type=f32)


def _dot_nt(a, b):
    return lax.dot_general(a, b, (((1,), (1,)), ((), ())), preferred_element_type=f32)


def _split2(x):
    hi = x.astype(bf16)
    lo = (x - hi.astype(f32)).astype(bf16)
    return hi, lo


def _split3(x):
    hi = x.astype(bf16)
    r1 = x - hi.astype(f32)
    mid = r1.astype(bf16)
    lo = (r1 - mid.astype(f32)).astype(bf16)
    return hi, mid, lo


def _dot_exact01(m01, x, parts):
    ps = _split3(x) if parts == 3 else _split2(x)
    acc = _dot(m01, ps[0])
    for p in ps[1:]:
        acc = acc + _dot(m01, p)
    return acc


def _segsum(x, seg):
    hi, lo = _split2(x)
    return _dot(hi, seg) + _dot(lo, seg)


def _sigmoid(x):
    return 1.0 / (1.0 + jnp.exp(-x))


def _silu(x):
    return x * _sigmoid(x)


def _inproj_body(x_ref, g_ref, w_ref, conv_ref, rest_ref):
    x = x_ref[...]
    ms = jnp.mean(x * x, axis=-1, keepdims=True)
    u = (x * lax.rsqrt(ms + NORM_EPS) * g_ref[...]).astype(bf16)
    for c0 in range(0, CONV_PAD, 256):
        conv_ref[:, c0:c0 + 256] = _dot(u, w_ref[:, c0:c0 + 256])
    for c0 in range(0, REST_COLS, 256):
        rest_ref[:, c0:c0 + 256] = _dot(u, w_ref[:, CONV_PAD + c0:CONV_PAD + c0 + 256])


def _inproj(h, norm_g, w_pad):
    T = h.shape[0]
    tm = TOKEN_TILE
    return pl.pallas_call(
        _inproj_body,
        grid=(T // tm,),
        in_specs=[
            pl.BlockSpec((tm, D_MODEL), lambda i: (i, 0)),
            pl.BlockSpec((1, D_MODEL), lambda i: (0, 0)),
            pl.BlockSpec((D_MODEL, CONV_PAD + REST_COLS), lambda i: (0, 0)),
        ],
        out_specs=[
            pl.BlockSpec((tm, CONV_PAD), lambda i: (i, 0)),
            pl.BlockSpec((tm, REST_COLS), lambda i: (i, 0)),
        ],
        out_shape=[
            jax.ShapeDtypeStruct((T, CONV_PAD), f32),
            jax.ShapeDtypeStruct((T, REST_COLS), f32),
        ],
        compiler_params=pltpu.CompilerParams(
            dimension_semantics=("arbitrary",), vmem_limit_bytes=VMEM_LIMIT),
        name="inproj",
    )(h, norm_g, w_pad)


def _rwkv_prep_body(tiles_per_seq, main_ref, prev_ref, next_ref, cw_ref, cb_ref, w0_ref, wup_ref,
                    a0_ref, aup_ref, kk_ref, ka_ref, rk_ref, seg_ref, tri_ref,
                    slab_ref, v_ref, bonus_ref):
    tm = main_ref.shape[0]
    i = pl.program_id(0)
    local = i % tiles_per_seq
    is_first = local == 0
    is_last = local == tiles_per_seq - 1

    xm = main_ref[...]
    row = lax.broadcasted_iota(jnp.int32, (tm, 1), 0)
    prev_row = jnp.where(is_first, 0.0, prev_ref[7:8, :])
    next_row = jnp.where(is_last, 0.0, next_ref[0:1, :])
    xp = jnp.where(row == 0, prev_row, pltpu.roll(xm, 1, 0))
    xn = jnp.where(row == tm - 1, next_row, pltpu.roll(xm, tm - 1, 0))
    cw = cw_ref[...]
    c = cb_ref[...] + xp * cw[0:1] + xm * cw[1:2] + xn * cw[2:3]

    r = c[:, 0:512]
    k = c[:, 512:1024]
    v = c[:, 1024:1536]
    zw = c[:, 1536:1664]
    za = c[:, 1664:1792]

    wl = w0_ref[...] + _dot(jnp.tanh(zw).astype(bf16), wup_ref[...])
    logw = -DECAY_SCALE * _sigmoid(wl)
    a = _sigmoid(a0_ref[...] + _dot(za.astype(bf16), aup_ref[...]))
    seg = seg_ref[...]
    kk = k * kk_ref[...]
    kk = kk * lax.rsqrt(_segsum(kk * kk, seg) + KK_EPS)
    kmod = k * (1.0 + (a - 1.0) * ka_ref[...])
    kka = kk * a
    bonus_ref[...] = _segsum(r * kmod * rk_ref[...], seg) * v
    v_ref[...] = v

    for d in range(2):
        lw = logw[:, d * 512:(d + 1) * 512]
        cums = _dot_exact01(tri_ref[d], lw, 3)
        cum_in = cums[:tm]
        rev_ex = cums[tm:]
        e_pos = jnp.exp(cum_in)
        e_neg = jnp.exp(-cum_in)
        e_rev = jnp.exp(rev_ex)
        slab_ref[d, 0] = r * e_pos
        slab_ref[d, 1] = kmod * e_neg
        slab_ref[d, 2] = -kk * jnp.exp(cum_in - lw)
        slab_ref[d, 3] = kka * e_neg
        slab_ref[d, 4] = kka * e_rev
        slab_ref[d, 5] = kmod * e_rev
        slab_ref[d, 6] = jnp.exp(cum_in + rev_ex)


def _rwkv_prep(conv_in, seq_len, prm):
    T = conv_in.shape[0]
    tm = TOKEN_TILE
    tiles_per_seq = seq_len // tm
    n8 = T // 8
    const = lambda shape: pl.BlockSpec(shape, lambda i: (0,) * len(shape))
    return pl.pallas_call(
        functools.partial(_rwkv_prep_body, tiles_per_seq),
        grid=(T // tm,),
        in_specs=[
            pl.BlockSpec((tm, CONV_PAD), lambda i: (i, 0)),
            pl.BlockSpec((8, CONV_PAD), lambda i: (jnp.maximum(i * (tm // 8) - 1, 0), 0)),
            pl.BlockSpec((8, CONV_PAD), lambda i: (jnp.minimum((i + 1) * (tm // 8), n8 - 1), 0)),
            const((3, CONV_PAD)), const((1, CONV_PAD)), const((1, 1024)), const((128, 1024)),
            const((1, 512)), const((128, 512)), const((1, 512)), const((1, 512)), const((1, 512)),
            const((512, 512)), const((2, 2 * tm, tm)),
        ],
        out_specs=[
            pl.BlockSpec((2, 7, tm, 512), lambda i: (0, 0, i, 0)),
            pl.BlockSpec((tm, 512), lambda i: (i, 0)),
            pl.BlockSpec((tm, 512), lambda i: (i, 0)),
        ],
        out_shape=[
            jax.ShapeDtypeStruct((2, 7, T, 512), f32),
            jax.ShapeDtypeStruct((T, 512), f32),
            jax.ShapeDtypeStruct((T, 512), f32),
        ],
        compiler_params=pltpu.CompilerParams(
            dimension_semantics=("arbitrary",), vmem_limit_bytes=VMEM_LIMIT),
        name="rwkv_prep",
    )(conv_in, conv_in, conv_in, prm["conv_w"], prm["conv_b"], prm["w0"], prm["w_up"],
      prm["a0"], prm["a_up"], prm["k_k"], prm["k_a"], prm["r_k"], prm["seg"], prm["tri"])


def _mm(a, b):
    return _dot(a.astype(bf16), b.astype(bf16))


def _stack_heads(x, head_masks):
    return jnp.concatenate([x * hm for hm in head_masks], axis=0)


def _rwkv_scan_body(slab_f_ref, slab_b_ref, v_f_ref, v_b_ref, y_f_ref, y_b_ref, z_ref):
    L = CHUNK
    n_groups = RW_WIDTH // GROUP

    @pl.when(pl.program_id(1) == 0)
    def _():
        z_ref[...] = jnp.zeros_like(z_ref)

    rs = lax.broadcasted_iota(jnp.int32, (STACK, STACK), 0)
    cs = lax.broadcasted_iota(jnp.int32, (STACK, STACK), 1)
    eye = rs == cs
    rt_idx = rs % L
    ct_idx = cs % L
    lane = lax.broadcasted_iota(jnp.int32, (1, GROUP), 1) // HEAD_DIM
    head_masks = [(lane == h).astype(f32) for h in range(HEADS_PER_GROUP)]
    strict = [(ct_idx < rt_idx).astype(f32), (ct_idx > rt_idx).astype(f32)]
    incl = [(ct_idx <= rt_idx).astype(f32), (ct_idx >= rt_idx).astype(f32)]

    chains = []
    for d, (slab_ref, v_ref) in enumerate(((slab_f_ref, v_f_ref), (slab_b_ref, v_b_ref))):
        for g in range(n_groups):
            lanes = slice(g * GROUP, (g + 1) * GROUP)
            chains.append(dict(
                d=d, g=g, lanes=lanes,
                r=_stack_heads(slab_ref[0, :, lanes], head_masks),
                k=_stack_heads(slab_ref[1, :, lanes], head_masks),
                a=_stack_heads(slab_ref[2, :, lanes], head_masks),
                b=_stack_heads(slab_ref[3, :, lanes], head_masks),
                bp=_stack_heads(slab_ref[4, :, lanes], head_masks),
                kp=_stack_heads(slab_ref[5, :, lanes], head_masks),
                pdiag=slab_ref[6, 0:1, lanes],
                v=_stack_heads(v_ref[:, lanes], head_masks),
            ))

    for ch in chains:
        d = ch["d"]
        gram = _dot_nt(jnp.concatenate([ch["a"], ch["r"]], axis=0).astype(bf16),
                       jnp.concatenate([ch["b"], ch["k"]], axis=0).astype(bf16))
        ch["m_ab"] = gram[:STACK, :STACK] * strict[d]
        ch["m_ak"] = gram[:STACK, STACK:] * strict[d]
        ch["m_rb"] = gram[STACK:, :STACK] * incl[d]
        ch["m_rk"] = gram[STACK:, STACK:] * incl[d]
    for ch in chains:
        ch["t"] = jnp.where(eye, 1.0, 0.0) + ch["m_ab"]
        ch["q"] = _mm(ch["m_ab"], ch["m_ab"])
    for _ in range(4):
        for ch in chains:
            p = _mm(jnp.concatenate([ch["t"], ch["q"]], axis=0), ch["q"])
            ch["t"] = ch["t"] + p[:STACK]
            ch["q"] = p[STACK:]
    for ch in chains:
        ch["t"] = ch["t"] + _mm(ch["t"], ch["q"])
    for ch in chains:
        ch["w1"] = _mm(ch["m_ak"], ch["v"])
    for ch in chains:
        ch["au"] = _mm(ch["t"], jnp.concatenate([ch["a"], ch["w1"]], axis=1))
    for ch in chains:
        o1 = _mm(ch["m_rb"], ch["au"])
        ch["r_hat"] = ch["r"] + o1[:, :GROUP]
        ch["y_hat"] = o1[:, GROUP:] + _mm(ch["m_rk"], ch["v"])
    for ch in chains:
        bp_t = ch["bp"].T
        kp_t = ch["kp"].T
        a_hat = ch["au"][:, :GROUP]
        u_hat = ch["au"][:, GROUP:]
        ch["phi_t"] = _mm(bp_t, a_hat) + jnp.where(eye, ch["pdiag"], 0.0)
        ch["psi_t"] = _mm(jnp.concatenate([bp_t, kp_t], axis=1), jnp.concatenate([u_hat, ch["v"]], axis=0))
    y_refs = (y_f_ref, y_b_ref)
    for ch in chains:
        z = z_ref[ch["d"], ch["g"]].astype(bf16)
        y_s = _dot(ch["r_hat"].astype(bf16), z) + ch["y_hat"]
        z_ref[ch["d"], ch["g"]] = _dot(ch["phi_t"].astype(bf16), z) + ch["psi_t"]
        y = y_s[0:L]
        for h in range(1, HEADS_PER_GROUP):
            y = y + y_s[h * L:(h + 1) * L]
        y_refs[ch["d"]][:, ch["lanes"]] = y


def _rwkv_scan(slab, v, batch, seq_len):
    T = v.shape[0]
    L = CHUNK
    nc = seq_len // L
    fwd = lambda b, c: b * nc + c
    bwd = lambda b, c: b * nc + (nc - 1 - c)
    return pl.pallas_call(
        _rwkv_scan_body,
        grid=(batch, nc),
        in_specs=[
            pl.BlockSpec((None, 7, L, 512), lambda b, c: (0, 0, fwd(b, c), 0)),
            pl.BlockSpec((None, 7, L, 512), lambda b, c: (1, 0, bwd(b, c), 0)),
            pl.BlockSpec((L, 512), lambda b, c: (fwd(b, c), 0)),
            pl.BlockSpec((L, 512), lambda b, c: (bwd(b, c), 0)),
        ],
        out_specs=[
            pl.BlockSpec((L, 512), lambda b, c: (fwd(b, c), 0)),
            pl.BlockSpec((L, 512), lambda b, c: (bwd(b, c), 0)),
        ],
        out_shape=[jax.ShapeDtypeStruct((T, 512), f32), jax.ShapeDtypeStruct((T, 512), f32)],
        scratch_shapes=[pltpu.VMEM((2, RW_WIDTH // GROUP, GROUP, GROUP), f32)],
        compiler_params=pltpu.CompilerParams(
            dimension_semantics=("arbitrary", "arbitrary"), vmem_limit_bytes=VMEM_LIMIT),
        name="rwkv_scan",
    )(slab, slab, v, v)


def _attn_prep_body(tiles_per_seq, rest_ref, qn_ref, kn_ref, seg_ref, q0_ref, q1_ref, k_ref, v_ref):
    tm = rest_ref.shape[0]
    seg = seg_ref[...]
    q = rest_ref[:, 512:1024]
    k = rest_ref[:, 1024:1536]
    v = rest_ref[:, 1536:2048]
    inv_d = 1.0 / HEAD_DIM
    qms = _segsum(q * q, seg) * inv_d
    kms = _segsum(k * k, seg) * inv_d
    qn = q * lax.rsqrt(qms + NORM_EPS) * qn_ref[...] * (HEAD_DIM ** -0.5)
    kn = k * lax.rsqrt(kms + NORM_EPS) * kn_ref[...]
    lane = lax.broadcasted_iota(jnp.int32, (1, DA_WIDTH), 1)
    first_map = (lane // HEAD_DIM) % 2 == 0
    q0 = jnp.where(first_map, qn, 0.0)
    q1 = jnp.where(first_map, 0.0, qn)

    pos = (pl.program_id(0) % tiles_per_seq) * tm + lax.broadcasted_iota(jnp.int32, (tm, 1), 0)
    hi = (pos // POS_SPLIT).astype(f32)
    lo = (pos % POS_SPLIT).astype(f32)
    al = lax.broadcasted_iota(jnp.int32, (1, DA_V_DIM), 1)
    ones_col = jnp.where(al == 0, 1.0, 0.0) + jnp.zeros((tm, DA_V_DIM), f32)
    for h in range(DA_HEADS):
        slope = 2.0 ** (-(ALIBI_MAX_BIAS / DA_HEADS) * (h + 1))
        q_aug = jnp.where(al == 0, -slope * POS_SPLIT * hi,
                          jnp.where(al == 1, -slope * lo, jnp.where(al < 4, 1.0, 0.0)))
        k_aug = jnp.where(al < 2, 1.0,
                          jnp.where(al == 2, slope * POS_SPLIT * hi, jnp.where(al == 3, slope * lo, 0.0)))
        src = slice(h * DA_V_DIM, (h + 1) * DA_V_DIM)
        main = slice(h * DA_EXT, h * DA_EXT + DA_V_DIM)
        aug = slice(h * DA_EXT + DA_V_DIM, (h + 1) * DA_EXT)
        q0_ref[:, main] = q0[:, src].astype(bf16)
        q1_ref[:, main] = q1[:, src].astype(bf16)
        k_ref[:, main] = kn[:, src].astype(bf16)
        v_ref[:, main] = v[:, src].astype(bf16)
        q0_ref[:, aug] = q_aug.astype(bf16)
        q1_ref[:, aug] = q_aug.astype(bf16)
        k_ref[:, aug] = k_aug.astype(bf16)
        v_ref[:, aug] = ones_col.astype(bf16)


def _attn_prep(rest, seq_len, prm):
    T = rest.shape[0]
    tm = TOKEN_TILE
    width = DA_HEADS * DA_EXT
    out = jax.ShapeDtypeStruct((T, width), bf16)
    spec = pl.BlockSpec((tm, width), lambda i: (i, 0))
    return pl.pallas_call(
        functools.partial(_attn_prep_body, seq_len // tm),
        grid=(T // tm,),
        in_specs=[
            pl.BlockSpec((tm, REST_COLS), lambda i: (i, 0)),
            pl.BlockSpec((1, DA_WIDTH), lambda i: (0, 0)),
            pl.BlockSpec((1, DA_WIDTH), lambda i: (0, 0)),
            pl.BlockSpec((512, 512), lambda i: (0, 0)),
        ],
        out_specs=[spec, spec, spec, spec],
        out_shape=[out, out, out, out],
        compiler_params=pltpu.CompilerParams(
            dimension_semantics=("arbitrary",), vmem_limit_bytes=VMEM_LIMIT),
        name="attn_prep",
    )(rest, prm["q_norm"], prm["k_norm"], prm["seg"])


def _flash_body(lam_init, tq, tk, seq_len, q0_ref, q1_ref, k_ref, v_ref, lam_ref, subln_ref, o_ref,
                m_ref, acc_ref):
    h = pl.program_id(1)
    qi = pl.program_id(2)
    nk = seq_len // tk

    m_ref[...] = jnp.full_like(m_ref, -jnp.inf)
    acc_ref[...] = jnp.zeros_like(acc_ref)

    is_aug = lax.broadcasted_iota(jnp.int32, (1, DA_EXT), 1) >= DA_V_DIM
    q_left = (q0_ref[...], q1_ref[...])
    q_right = tuple(jnp.where(is_aug, -q, q) for q in q_left)
    q_diag = tuple(jnp.where(is_aug, jnp.zeros_like(q), q) for q in q_left)

    def tile(kt, qs, bias):
        start = pl.multiple_of(kt * tk, tk)
        k = k_ref[pl.ds(start, tk), :]
        v = v_ref[pl.ds(start, tk), :]
        for mi in range(2):
            s = _dot_nt(qs[mi], k)
            if bias is not None:
                s = s - bias
            m_prev = m_ref[mi]
            m_new = jnp.maximum(m_prev, jnp.max(s, axis=-1, keepdims=True))
            alpha = jnp.exp(m_prev - m_new)
            p = jnp.exp(s - m_new).astype(bf16)
            acc_ref[mi] = alpha * acc_ref[mi] + _dot(p, v)
            m_ref[mi] = m_new

    def loop(lo, hi, qs):
        def body(kt, carry):
            tile(kt, qs, None)
            return carry
        lax.fori_loop(lo, hi, body, 0)

    kd = (qi * tq) // tk
    loop(0, kd, q_left)
    slope = jnp.exp2(jnp.full((1, 1), -(ALIBI_MAX_BIAS / DA_HEADS), f32) * (h + 1).astype(f32))
    rel = (lax.broadcasted_iota(jnp.int32, (tq, tk), 0) - lax.broadcasted_iota(jnp.int32, (tq, tk), 1)
           + (qi * tq - kd * tk))
    tile(kd, q_diag, slope * jnp.abs(rel).astype(f32))
    loop(kd + 1, nk, q_right)

    lv = lam_ref[...]
    lam = (jnp.exp(jnp.sum(lv[0:1] * lv[1:2], axis=-1, keepdims=True))
           - jnp.exp(jnp.sum(lv[2:3] * lv[3:4], axis=-1, keepdims=True)) + lam_init)
    acc0 = acc_ref[0]
    acc1 = acc_ref[1]
    o = (acc0[:, :DA_V_DIM] / acc0[:, DA_V_DIM:DA_V_DIM + 1]
         - lam * (acc1[:, :DA_V_DIM] / acc1[:, DA_V_DIM:DA_V_DIM + 1]))
    ms = jnp.mean(o * o, axis=-1, keepdims=True)
    o_ref[...] = o * lax.rsqrt(ms + NORM_EPS) * subln_ref[...] * (1.0 - lam_init)


def _flash(q0, q1, k, v, lam_vec, subln, lam_init, batch, seq_len):
    T = q0.shape[0]
    tq = min(FLASH_TQ, seq_len)
    tk = min(FLASH_TK, seq_len)
    nq = seq_len // tq
    qspec = pl.BlockSpec((tq, DA_EXT), lambda b, h, i: (b * nq + i, h))
    kspec = pl.BlockSpec((seq_len, DA_EXT), lambda b, h, i: (b, h))
    return pl.pallas_call(
        functools.partial(_flash_body, lam_init, tq, tk, seq_len),
        grid=(batch, DA_HEADS, nq),
        in_specs=[qspec, qspec, kspec, kspec,
                  pl.BlockSpec((4, HEAD_DIM), lambda b, h, i: (0, 0)),
                  pl.BlockSpec((1, DA_V_DIM), lambda b, h, i: (0, 0))],
        out_specs=pl.BlockSpec((tq, DA_V_DIM), lambda b, h, i: (b * nq + i, h)),
        out_shape=jax.ShapeDtypeStruct((T, DA_WIDTH), f32),
        scratch_shapes=[pltpu.VMEM((2, tq, 1), f32), pltpu.VMEM((2, tq, DA_EXT), f32)],
        compiler_params=pltpu.CompilerParams(
            dimension_semantics=("arbitrary", "arbitrary", "arbitrary"),
            vmem_limit_bytes=VMEM_LIMIT),
        name="flash_diff_attn",
    )(q0, q1, k, v, lam_vec, subln)


def _out_body(h_ref, yf_ref, yb_ref, bonus_ref, rest_ref, o_ref, p_ref, seg_ref, lng_ref, lnb_ref,
              wout_ref, pproj_ref, pnorm_ref, gw_ref, gb_ref, out_ref):
    seg = seg_ref[...]
    inv_n = 1.0 / HEAD_DIM
    y = yf_ref[...] + yb_ref[...]
    mu = _segsum(y, seg) * inv_n
    yc = y - mu
    var = _segsum(yc * yc, seg) * inv_n
    y_rw = (yc * lax.rsqrt(var + LN_X_EPS) * lng_ref[...] + lnb_ref[...] + bonus_ref[...])
    y_rw = y_rw * _silu(rest_ref[:, 0:512])
    y_da = o_ref[...] * _silu(rest_ref[:, 2048:2560])
    h1 = (h_ref[...] + _dot(y_rw.astype(bf16), wout_ref[0:512, :])
          + _dot(y_da.astype(bf16), wout_ref[512:1024, :]))
    e = _dot(p_ref[...].astype(bf16), pproj_ref[...])
    e = e * lax.rsqrt(jnp.mean(e * e, axis=-1, keepdims=True) + NORM_EPS) * pnorm_ref[...]
    gate = _sigmoid(_dot(h1.astype(bf16), gw_ref[...]) + gb_ref[...])
    out_ref[...] = h1 + gate * e


def _out_stage(h, y_f, y_b, bonus, rest, o, p, prm):
    T = h.shape[0]
    tm = TOKEN_TILE
    row = lambda w: pl.BlockSpec((tm, w), lambda i: (i, 0))
    const = lambda shape: pl.BlockSpec(shape, lambda i: (0,) * len(shape))
    return pl.pallas_call(
        _out_body,
        grid=(T // tm,),
        in_specs=[row(D_MODEL), row(512), row(512), row(512), row(REST_COLS), row(512), row(D_PLE),
                  const((512, 512)), const((1, 512)), const((1, 512)),
                  const((D_MODEL, D_MODEL)), const((D_PLE, D_MODEL)), const((1, D_MODEL)),
                  const((D_MODEL, D_MODEL)), const((1, D_MODEL))],
        out_specs=row(D_MODEL),
        out_shape=jax.ShapeDtypeStruct((T, D_MODEL), f32),
        compiler_params=pltpu.CompilerParams(
            dimension_semantics=("arbitrary",), vmem_limit_bytes=VMEM_LIMIT),
        name="out_stage",
    )(h, y_f, y_b, bonus, rest, o, p, prm["seg"], prm["ln_g"], prm["ln_b"], prm["w_out"],
      prm["ple_proj"], prm["ple_norm"], prm["gate_w"], prm["gate_b"])


def _chunk_triangles(tm):
    t = np.arange(tm)
    same = (t[:, None] // CHUNK) == (t[None, :] // CHUNK)
    low_incl = same & (t[None, :] <= t[:, None])
    up_strict = same & (t[None, :] > t[:, None])
    up_incl = same & (t[None, :] >= t[:, None])
    low_strict = same & (t[None, :] < t[:, None])
    tri = np.stack([np.concatenate([low_incl, up_strict], 0), np.concatenate([up_incl, low_strict], 0)])
    return jnp.asarray(tri.astype(np.float32), dtype=bf16)


def _layer_params(i, norm_pre, w_in, w_out, rw_conv_w, rw_conv_b, rw_w0, rw_w_up, rw_a0, rw_a_up, rw_k_k,
                  rw_k_a, rw_r_k, rw_ln_g, rw_ln_b, da_q_norm, da_k_norm, da_lambda, da_subln, ple_proj,
                  ple_norm, ple_gate_w, ple_gate_b):
    pad = CONV_PAD - (CONV_COLS)
    w = w_in[i]
    w_pad = jnp.concatenate(
        [w[:, :CONV_COLS], jnp.zeros((D_MODEL, pad), f32), w[:, CONV_COLS:]], axis=1).astype(bf16)
    wup = rw_w_up[i]
    zeros = jnp.zeros((W_LORA, RW_WIDTH), f32)
    w_up_bd = jnp.concatenate(
        [jnp.concatenate([wup[0], zeros], 1), jnp.concatenate([zeros, wup[1]], 1)], 0).astype(bf16)
    a_up_pad = jnp.concatenate([rw_a_up[i], jnp.zeros((64, RW_WIDTH), f32)], 0).astype(bf16)
    lane = np.arange(512)
    seg = jnp.asarray((lane[:, None] // HEAD_DIM == lane[None, :] // HEAD_DIM).astype(np.float32), dtype=bf16)
    return dict(
        norm_pre=norm_pre[i].reshape(1, D_MODEL),
        w_in=w_pad,
        conv_w=jnp.pad(rw_conv_w[i], ((0, 0), (0, pad))),
        conv_b=jnp.pad(rw_conv_b[i], (0, pad)).reshape(1, CONV_PAD),
        w0=rw_w0[i].reshape(1, 2 * RW_WIDTH),
        w_up=w_up_bd,
        a0=rw_a0[i].reshape(1, RW_WIDTH),
        a_up=a_up_pad,
        k_k=rw_k_k[i].reshape(1, RW_WIDTH),
        k_a=rw_k_a[i].reshape(1, RW_WIDTH),
        r_k=rw_r_k[i].reshape(1, RW_WIDTH),
        ln_g=rw_ln_g[i].reshape(1, RW_WIDTH),
        ln_b=rw_ln_b[i].reshape(1, RW_WIDTH),
        q_norm=jnp.tile(da_q_norm[i].reshape(1, 2 * HEAD_DIM), (1, DA_HEADS)),
        k_norm=jnp.tile(da_k_norm[i].reshape(1, 2 * HEAD_DIM), (1, DA_HEADS)),
        lam_vec=da_lambda[i],
        subln=da_subln[i].reshape(1, DA_V_DIM),
        w_out=w_out[i].astype(bf16),
        ple_proj=ple_proj[i].astype(bf16),
        ple_norm=ple_norm[i].reshape(1, D_MODEL),
        gate_w=ple_gate_w[i].astype(bf16),
        gate_b=ple_gate_b[i].reshape(1, D_MODEL),
        seg=seg,
        tri=_chunk_triangles(TOKEN_TILE),
    )


def _layer(h, p, lam_init, prm, batch, seq_len):
    conv_in, rest = _inproj(h, prm["norm_pre"], prm["w_in"])
    slab, v_rw, bonus = _rwkv_prep(conv_in, seq_len, prm)
    y_f, y_b = _rwkv_scan(slab, v_rw, batch, seq_len)
    q0, q1, k, v = _attn_prep(rest, seq_len, prm)
    o = _flash(q0, q1, k, v, prm["lam_vec"], prm["subln"], lam_init, batch, seq_len)
    return _out_stage(h, y_f, y_b, bonus, rest, o, p, prm)


def _trunk(x, p, layers):
    batch, seq_len, _ = x.shape
    h = x.reshape(batch * seq_len, D_MODEL)
    for i, prm in enumerate(layers):
        lam_init = 0.8 - 0.6 * math.exp(-0.3 * i)
        h = _layer(h, p[i].reshape(batch * seq_len, D_PLE), lam_init, prm, batch, seq_len)
    return h.reshape(batch, seq_len, D_MODEL)


def kernel(x_prompt, x_sample, p_prompt, p_sample, norm_pre, w_in, w_out, rw_conv_w, rw_conv_b, rw_w0, rw_w_up, rw_a0, rw_a_up, rw_k_k, rw_k_a, rw_r_k, rw_ln_g, rw_ln_b, da_q_norm, da_k_norm, da_lambda, da_subln, ple_proj, ple_norm, ple_gate_w, ple_gate_b):
    depth = norm_pre.shape[0]
    layers = [_layer_params(i, norm_pre, w_in, w_out, rw_conv_w, rw_conv_b, rw_w0, rw_w_up, rw_a0, rw_a_up,
                            rw_k_k, rw_k_a, rw_r_k, rw_ln_g, rw_ln_b, da_q_norm, da_k_norm, da_lambda,
                            da_subln, ple_proj, ple_norm, ple_gate_w, ple_gate_b) for i in range(depth)]
    return (_trunk(x_prompt, p_prompt, layers), _trunk(x_sample, p_sample, layers))
```

```python
import functools
import math

import numpy as np
import jax
import jax.numpy as jnp
from jax import lax
from jax.experimental import pallas as pl
from jax.experimental.pallas import tpu as pltpu

f32 = jnp.float32
bf16 = jnp.bfloat16

D_MODEL = 1024
D_PLE = 256
RW_WIDTH = 512
HEAD_DIM = 64
W_LORA = 64
CONV_COLS = 3 * RW_WIDTH + 2 * W_LORA + 64
CONV_PAD = 1792
DA_HEADS = 4
DA_V_DIM = 128
DA_WIDTH = 512
DA_EXT = 256
REST_COLS = 5 * 512
NORM_EPS = 1e-6
LN_X_EPS = 64e-5
KK_EPS = 1e-12
ALIBI_MAX_BIAS = 8.0
DECAY_SCALE = math.exp(-0.5)
LOG2_E = math.log2(math.e)
POS_SPLIT = 128

CHUNK = 64
GROUP = 256
HEADS_PER_GROUP = GROUP // HEAD_DIM
STACK = HEADS_PER_GROUP * CHUNK
TOKEN_TILE = 256
FLASH_TQ = 256
FLASH_TK = 512
VMEM_LIMIT = 48 * 1024 * 1024


def _dot(a, b):
    return jnp.dot(a, b, preferred_element_type=f32)


def _dot_nt(a, b):
    return lax.dot_general(a, b, (((1,), (1,)), ((), ())), preferred_element_type=f32)


def _split2(x):
    hi = x.astype(bf16)
    lo = (x - hi.astype(f32)).astype(bf16)
    return hi, lo


def _split3(x):
    hi = x.astype(bf16)
    r1 = x - hi.astype(f32)
    mid = r1.astype(bf16)
    lo = (r1 - mid.astype(f32)).astype(bf16)
    return hi, mid, lo


def _dot_exact01(m01, x, parts):
    ps = _split3(x) if parts == 3 else _split2(x)
    acc = _dot(m01, ps[0])
    for p in ps[1:]:
        acc = acc + _dot(m01, p)
    return acc


def _segsum(x, seg):
    hi, lo = _split2(x)
    return _dot(hi, seg) + _dot(lo, seg)


def _sigmoid(x):
    return 1.0 / (1.0 + jnp.exp(-x))


def _silu(x):
    return x * _sigmoid(x)


def _inproj_body(x_ref, g_ref, w_ref, conv_ref, rest_ref):
    x = x_ref[...]
    ms = jnp.mean(x * x, axis=-1, keepdims=True)
    u = (x * lax.rsqrt(ms + NORM_EPS) * g_ref[...]).astype(bf16)
    for c0 in range(0, CONV_PAD, 256):
        conv_ref[:, c0:c0 + 256] = _dot(u, w_ref[:, c0:c0 + 256])
    for c0 in range(0, REST_COLS, 256):
        rest_ref[:, c0:c0 + 256] = _dot(u, w_ref[:, CONV_PAD + c0:CONV_PAD + c0 + 256])


def _inproj(h, norm_g, w_pad):
    T = h.shape[0]
    tm = TOKEN_TILE
    return pl.pallas_call(
        _inproj_body,
        grid=(T // tm,),
        in_specs=[
            pl.BlockSpec((tm, D_MODEL), lambda i: (i, 0)),
            pl.BlockSpec((1, D_MODEL), lambda i: (0, 0)),
            pl.BlockSpec((D_MODEL, CONV_PAD + REST_COLS), lambda i: (0, 0)),
        ],
        out_specs=[
            pl.BlockSpec((tm, CONV_PAD), lambda i: (i, 0)),
            pl.BlockSpec((tm, REST_COLS), lambda i: (i, 0)),
        ],
        out_shape=[
            jax.ShapeDtypeStruct((T, CONV_PAD), f32),
            jax.ShapeDtypeStruct((T, REST_COLS), f32),
        ],
        compiler_params=pltpu.CompilerParams(
            dimension_semantics=("arbitrary",), vmem_limit_bytes=VMEM_LIMIT),
        name="inproj",
    )(h, norm_g, w_pad)


def _rwkv_prep_body(tiles_per_seq, main_ref, prev_ref, next_ref, cw_ref, cb_ref, w0_ref, wup_ref,
                    a0_ref, aup_ref, kk_ref, ka_ref, rk_ref, seg_ref, tri_ref,
                    slab_ref, v_ref, bonus_ref):
    tm = main_ref.shape[0]
    i = pl.program_id(0)
    local = i % tiles_per_seq
    is_first = local == 0
    is_last = local == tiles_per_seq - 1

    xm = main_ref[...]
    row = lax.broadcasted_iota(jnp.int32, (tm, 1), 0)
    prev_row = jnp.where(is_first, 0.0, prev_ref[7:8, :])
    next_row = jnp.where(is_last, 0.0, next_ref[0:1, :])
    xp = jnp.where(row == 0, prev_row, pltpu.roll(xm, 1, 0))
    xn = jnp.where(row == tm - 1, next_row, pltpu.roll(xm, tm - 1, 0))
    cw = cw_ref[...]
    c = cb_ref[...] + xp * cw[0:1] + xm * cw[1:2] + xn * cw[2:3]

    r = c[:, 0:512]
    k = c[:, 512:1024]
    v = c[:, 1024:1536]
    zw = c[:, 1536:1664]
    za = c[:, 1664:1792]

    wl = w0_ref[...] + _dot(jnp.tanh(zw).astype(bf16), wup_ref[...])
    logw = -DECAY_SCALE * _sigmoid(wl)
    a = _sigmoid(a0_ref[...] + _dot(za.astype(bf16), aup_ref[...]))
    seg = seg_ref[...]
    kk = k * kk_ref[...]
    kk = kk * lax.rsqrt(_segsum(kk * kk, seg) + KK_EPS)
    kmod = k * (1.0 + (a - 1.0) * ka_ref[...])
    kka = kk * a
    bonus_ref[...] = _segsum(r * kmod * rk_ref[...], seg) * v
    v_ref[...] = v

    for d in range(2):
        lw = logw[:, d * 512:(d + 1) * 512]
        cums = _dot_exact01(tri_ref[d], lw, 3)
        cum_in = cums[:tm]
        rev_ex = cums[tm:]
        e_pos = jnp.exp(cum_in)
        e_neg = jnp.exp(-cum_in)
        e_rev = jnp.exp(rev_ex)
        slab_ref[d, 0] = r * e_pos
        slab_ref[d, 1] = kmod * e_neg
        slab_ref[d, 2] = -kk * jnp.exp(cum_in - lw)
        slab_ref[d, 3] = kka * e_neg
        slab_ref[d, 4] = kka * e_rev
        slab_ref[d, 5] = kmod * e_rev
        slab_ref[d, 6] = jnp.exp(cum_in + rev_ex)


def _rwkv_prep(conv_in, seq_len, prm):
    T = conv_in.shape[0]
    tm = TOKEN_TILE
    tiles_per_seq = seq_len // tm
    n8 = T // 8
    const = lambda shape: pl.BlockSpec(shape, lambda i: (0,) * len(shape))
    return pl.pallas_call(
        functools.partial(_rwkv_prep_body, tiles_per_seq),
        grid=(T // tm,),
        in_specs=[
            pl.BlockSpec((tm, CONV_PAD), lambda i: (i, 0)),
            pl.BlockSpec((8, CONV_PAD), lambda i: (jnp.maximum(i * (tm // 8) - 1, 0), 0)),
            pl.BlockSpec((8, CONV_PAD), lambda i: (jnp.minimum((i + 1) * (tm // 8), n8 - 1), 0)),
            const((3, CONV_PAD)), const((1, CONV_PAD)), const((1, 1024)), const((128, 1024)),
            const((1, 512)), const((128, 512)), const((1, 512)), const((1, 512)), const((1, 512)),
            const((512, 512)), const((2, 2 * tm, tm)),
        ],
        out_specs=[
            pl.BlockSpec((2, 7, tm, 512), lambda i: (0, 0, i, 0)),
            pl.BlockSpec((tm, 512), lambda i: (i, 0)),
            pl.BlockSpec((tm, 512), lambda i: (i, 0)),
        ],
        out_shape=[
            jax.ShapeDtypeStruct((2, 7, T, 512), f32),
            jax.ShapeDtypeStruct((T, 512), f32),
            jax.ShapeDtypeStruct((T, 512), f32),
        ],
        compiler_params=pltpu.CompilerParams(
            dimension_semantics=("arbitrary",), vmem_limit_bytes=VMEM_LIMIT),
        name="rwkv_prep",
    )(conv_in, conv_in, conv_in, prm["conv_w"], prm["conv_b"], prm["w0"], prm["w_up"],
      prm["a0"], prm["a_up"], prm["k_k"], prm["k_a"], prm["r_k"], prm["seg"], prm["tri"])


def _mm(a, b):
    return _dot(a.astype(bf16), b.astype(bf16))


def _stack_heads(x, head_masks):
    return jnp.concatenate([x * hm for hm in head_masks], axis=0)


def _rwkv_scan_body(slab_f_ref, slab_b_ref, v_f_ref, v_b_ref, y_f_ref, y_b_ref, z_ref):
    L = CHUNK
    n_groups = RW_WIDTH // GROUP

    @pl.when(pl.program_id(1) == 0)
    def _():
        z_ref[...] = jnp.zeros_like(z_ref)

    rs = lax.broadcasted_iota(jnp.int32, (STACK, STACK), 0)
    cs = lax.broadcasted_iota(jnp.int32, (STACK, STACK), 1)
    eye = rs == cs
    rt_idx = rs % L
    ct_idx = cs % L
    lane = lax.broadcasted_iota(jnp.int32, (1, GROUP), 1) // HEAD_DIM
    head_masks = [(lane == h).astype(f32) for h in range(HEADS_PER_GROUP)]
    strict = [(ct_idx < rt_idx).astype(f32), (ct_idx > rt_idx).astype(f32)]
    incl = [(ct_idx <= rt_idx).astype(f32), (ct_idx >= rt_idx).astype(f32)]

    chains = []
    for d, (slab_ref, v_ref) in enumerate(((slab_f_ref, v_f_ref), (slab_b_ref, v_b_ref))):
        for g in range(n_groups):
            lanes = slice(g * GROUP, (g + 1) * GROUP)
            chains.append(dict(
                d=d, g=g, lanes=lanes,
                r=_stack_heads(slab_ref[0, :, lanes], head_masks),
                k=_stack_heads(slab_ref[1, :, lanes], head_masks),
                a=_stack_heads(slab_ref[2, :, lanes], head_masks),
                b=_stack_heads(slab_ref[3, :, lanes], head_masks),
                bp=_stack_heads(slab_ref[4, :, lanes], head_masks),
                kp=_stack_heads(slab_ref[5, :, lanes], head_masks),
                pdiag=slab_ref[6, 0:1, lanes],
                v=_stack_heads(v_ref[:, lanes], head_masks),
            ))

    for ch in chains:
        d = ch["d"]
        gram = _dot_nt(jnp.concatenate([ch["a"], ch["r"]], axis=0).astype(bf16),
                       jnp.concatenate([ch["b"], ch["k"]], axis=0).astype(bf16))
        ch["m_ab"] = gram[:STACK, :STACK] * strict[d]
        ch["m_ak"] = gram[:STACK, STACK:] * strict[d]
        ch["m_rb"] = gram[STACK:, :STACK] * incl[d]
        ch["m_rk"] = gram[STACK:, STACK:] * incl[d]
    for ch in chains:
        ch["t"] = jnp.where(eye, 1.0, 0.0) + ch["m_ab"]
        ch["q"] = _mm(ch["m_ab"], ch["m_ab"])
    for _ in range(4):
        for ch in chains:
            p = _mm(jnp.concatenate([ch["t"], ch["q"]], axis=0), ch["q"])
            ch["t"] = ch["t"] + p[:STACK]
            ch["q"] = p[STACK:]
    for ch in chains:
        ch["t"] = ch["t"] + _mm(ch["t"], ch["q"])
    for ch in chains:
        ch["w1"] = _mm(ch["m_ak"], ch["v"])
    for ch in chains:
        ch["au"] = _mm(ch["t"], jnp.concatenate([ch["a"], ch["w1"]], axis=1))
    for ch in chains:
        o1 = _mm(ch["m_rb"], ch["au"])
        ch["r_hat"] = ch["r"] + o1[:, :GROUP]
        ch["y_hat"] = o1[:, GROUP:] + _mm(ch["m_rk"], ch["v"])
    for ch in chains:
        bp_t = ch["bp"].T
        kp_t = ch["kp"].T
        a_hat = ch["au"][:, :GROUP]
        u_hat = ch["au"][:, GROUP:]
        ch["phi_t"] = _mm(bp_t, a_hat) + jnp.where(eye, ch["pdiag"], 0.0)
        ch["psi_t"] = _mm(jnp.concatenate([bp_t, kp_t], axis=1), jnp.concatenate([u_hat, ch["v"]], axis=0))
    y_refs = (y_f_ref, y_b_ref)
    for ch in chains:
        z = z_ref[ch["d"], ch["g"]].astype(bf16)
        y_s = _dot(ch["r_hat"].astype(bf16), z) + ch["y_hat"]
        z_ref[ch["d"], ch["g"]] = _dot(ch["phi_t"].astype(bf16), z) + ch["psi_t"]
        y = y_s[0:L]
        for h in range(1, HEADS_PER_GROUP):
            y = y + y_s[h * L:(h + 1) * L]
        y_refs[ch["d"]][:, ch["lanes"]] = y


def _rwkv_scan(slab, v, batch, seq_len):
    T = v.shape[0]
    L = CHUNK
    nc = seq_len // L
    fwd = lambda b, c: b * nc + c
    bwd = lambda b, c: b * nc + (nc - 1 - c)
    return pl.pallas_call(
        _rwkv_scan_body,
        grid=(batch, nc),
        in_specs=[
            pl.BlockSpec((None, 7, L, 512), lambda b, c: (0, 0, fwd(b, c), 0)),
            pl.BlockSpec((None, 7, L, 512), lambda b, c: (1, 0, bwd(b, c), 0)),
            pl.BlockSpec((L, 512), lambda b, c: (fwd(b, c), 0)),
            pl.BlockSpec((L, 512), lambda b, c: (bwd(b, c), 0)),
        ],
        out_specs=[
            pl.BlockSpec((L, 512), lambda b, c: (fwd(b, c), 0)),
            pl.BlockSpec((L, 512), lambda b, c: (bwd(b, c), 0)),
        ],
        out_shape=[jax.ShapeDtypeStruct((T, 512), f32), jax.ShapeDtypeStruct((T, 512), f32)],
        scratch_shapes=[pltpu.VMEM((2, RW_WIDTH // GROUP, GROUP, GROUP), f32)],
        compiler_params=pltpu.CompilerParams(
            dimension_semantics=("arbitrary", "arbitrary"), vmem_limit_bytes=VMEM_LIMIT),
        name="rwkv_scan",
    )(slab, slab, v, v)


def _attn_prep_body(tiles_per_seq, rest_ref, qn_ref, kn_ref, seg_ref, q0_ref, q1_ref, k_ref, v_ref):
    tm = rest_ref.shape[0]
    seg = seg_ref[...]
    q = rest_ref[:, 512:1024]
    k = rest_ref[:, 1024:1536]
    v = rest_ref[:, 1536:2048]
    inv_d = 1.0 / HEAD_DIM
    qms = _segsum(q * q, seg) * inv_d
    kms = _segsum(k * k, seg) * inv_d
    qn = q * lax.rsqrt(qms + NORM_EPS) * qn_ref[...] * (HEAD_DIM ** -0.5)
    kn = k * lax.rsqrt(kms + NORM_EPS) * kn_ref[...]
    lane = lax.broadcasted_iota(jnp.int32, (1, DA_WIDTH), 1)
    first_map = (lane // HEAD_DIM) % 2 == 0
    q0 = jnp.where(first_map, qn, 0.0)
    q1 = jnp.where(first_map, 0.0, qn)

    pos = (pl.program_id(0) % tiles_per_seq) * tm + lax.broadcasted_iota(jnp.int32, (tm, 1), 0)
    hi = (pos // POS_SPLIT).astype(f32)
    lo = (pos % POS_SPLIT).astype(f32)
    al = lax.broadcasted_iota(jnp.int32, (1, DA_V_DIM), 1)
    ones_col = jnp.where(al == 0, 1.0, 0.0) + jnp.zeros((tm, DA_V_DIM), f32)
    for h in range(DA_HEADS):
        slope = 2.0 ** (-(ALIBI_MAX_BIAS / DA_HEADS) * (h + 1))
        q_aug = jnp.where(al == 0, -slope * POS_SPLIT * hi,
                          jnp.where(al == 1, -slope * lo, jnp.where(al < 4, 1.0, 0.0)))
        k_aug = jnp.where(al < 2, 1.0,
                          jnp.where(al == 2, slope * POS_SPLIT * hi, jnp.where(al == 3, slope * lo, 0.0)))
        src = slice(h * DA_V_DIM, (h + 1) * DA_V_DIM)
        main = slice(h * DA_EXT, h * DA_EXT + DA_V_DIM)
        aug = slice(h * DA_EXT + DA_V_DIM, (h + 1) * DA_EXT)
        q0_ref[:, main] = q0[:, src].astype(bf16)
        q1_ref[:, main] = q1[:, src].astype(bf16)
        k_ref[:, main] = kn[:, src].astype(bf16)
        v_ref[:, main] = v[:, src].astype(bf16)
        q0_ref[:, aug] = q_aug.astype(bf16)
        q1_ref[:, aug] = q_aug.astype(bf16)
        k_ref[:, aug] = k_aug.astype(bf16)
        v_ref[:, aug] = ones_col.astype(bf16)


def _attn_prep(rest, seq_len, prm):
    T = rest.shape[0]
    tm = TOKEN_TILE
    width = DA_HEADS * DA_EXT
    out = jax.ShapeDtypeStruct((T, width), bf16)
    spec = pl.BlockSpec((tm, width), lambda i: (i, 0))
    return pl.pallas_call(
        functools.partial(_attn_prep_body, seq_len // tm),
        grid=(T // tm,),
        in_specs=[
            pl.BlockSpec((tm, REST_COLS), lambda i: (i, 0)),
            pl.BlockSpec((1, DA_WIDTH), lambda i: (0, 0)),
            pl.BlockSpec((1, DA_WIDTH), lambda i: (0, 0)),
            pl.BlockSpec((512, 512), lambda i: (0, 0)),
        ],
        out_specs=[spec, spec, spec, spec],
        out_shape=[out, out, out, out],
        compiler_params=pltpu.CompilerParams(
            dimension_semantics=("arbitrary",), vmem_limit_bytes=VMEM_LIMIT),
        name="attn_prep",
    )(rest, prm["q_norm"], prm["k_norm"], prm["seg"])


def _flash_body(lam_init, tq, tk, seq_len, q0_ref, q1_ref, k_ref, v_ref, lam_ref, subln_ref, o_ref,
                s_ref, acc_ref):
    h = pl.program_id(1)
    qi = pl.program_id(2)
    nk = seq_len // tk
    lane_blocks = tk // DA_V_DIM

    kd = (qi * tq) // tk
    q_main = (q0_ref[:, :DA_V_DIM], q1_ref[:, :DA_V_DIM])
    q_aug = q0_ref[:, DA_V_DIM:]

    def tile_start(kt):
        return kt * tk if isinstance(kt, int) else pl.multiple_of(kt * tk, tk)

    def scores(kt, bias, running):
        k = k_ref[pl.ds(tile_start(kt), tk), :]
        if bias is None:
            sign = jnp.where(kt < kd, 1.0, jnp.where(kt > kd, -1.0, 0.0)).astype(f32)
            aug = (q_aug.astype(f32) * sign).astype(bf16)
        else:
            aug = jnp.zeros_like(q_aug)
        out = []
        for mi in range(2):
            s = _dot_nt(jnp.concatenate([q_main[mi], aug], axis=1), k)
            if bias is not None:
                s = s - bias
            s = s * LOG2_E
            s_ref[mi, kt] = s
            mx = running[mi]
            for c in range(lane_blocks):
                mx = jnp.maximum(mx, s[:, c * DA_V_DIM:(c + 1) * DA_V_DIM])
            out.append(mx if bias is not None else jnp.where(kt == kd, running[mi], mx))
        return tuple(out)

    running = (jnp.full((tq, DA_V_DIM), -jnp.inf, f32),) * 2
    running = lax.fori_loop(0, nk // 2, lambda j, r: scores(2 * j + 1, None, scores(2 * j, None, r)), running)
    if nk % 2:
        running = scores(nk - 1, None, running)
    slope = jnp.exp2(jnp.full((1, 1), -(ALIBI_MAX_BIAS / DA_HEADS), f32) * (h + 1).astype(f32))
    rel = (lax.broadcasted_iota(jnp.int32, (tq, tk), 0) - lax.broadcasted_iota(jnp.int32, (tq, tk), 1)
           + (qi * tq - kd * tk))
    running = scores(kd, slope * jnp.abs(rel).astype(f32), running)
    row_max = [jnp.broadcast_to(jnp.max(mx, axis=-1, keepdims=True), (tq, DA_V_DIM)) for mx in running]

    acc_ref[...] = jnp.zeros_like(acc_ref)

    def weighted_values(mi, first, count):
        v = v_ref[pl.ds(tile_start(first), count * tk), :]
        ps = [jnp.exp2(s_ref[mi, first + j, :, c * DA_V_DIM:(c + 1) * DA_V_DIM] - row_max[mi]).astype(bf16)
              for j in range(count) for c in range(lane_blocks)]
        return _dot(jnp.concatenate(ps, axis=1), v)

    def quad(j, carry):
        for mi in range(2):
            acc_ref[mi] += weighted_values(mi, 4 * j, 2) + weighted_values(mi, 4 * j + 2, 2)
        return carry

    lax.fori_loop(0, nk // 4, quad, 0)
    done = 4 * (nk // 4)
    for count in (2, 1):
        if nk - done >= count:
            for mi in range(2):
                acc_ref[mi] += weighted_values(mi, done, count)
            done += count

    lv = lam_ref[...]
    lam = (jnp.exp(jnp.sum(lv[0:1] * lv[1:2], axis=-1, keepdims=True))
           - jnp.exp(jnp.sum(lv[2:3] * lv[3:4], axis=-1, keepdims=True)) + lam_init)
    acc0 = acc_ref[0]
    acc1 = acc_ref[1]
    o = (acc0[:, :DA_V_DIM] / acc0[:, DA_V_DIM:DA_V_DIM + 1]
         - lam * (acc1[:, :DA_V_DIM] / acc1[:, DA_V_DIM:DA_V_DIM + 1]))
    ms = jnp.mean(o * o, axis=-1, keepdims=True)
    o_ref[...] = o * lax.rsqrt(ms + NORM_EPS) * subln_ref[...] * (1.0 - lam_init)


def _flash(q0, q1, k, v, lam_vec, subln, lam_init, batch, seq_len):
    T = q0.shape[0]
    tq = min(FLASH_TQ, seq_len)
    tk = min(FLASH_TK, seq_len)
    nq = seq_len // tq
    qspec = pl.BlockSpec((tq, DA_EXT), lambda b, h, i: (b * nq + i, h))
    kspec = pl.BlockSpec((seq_len, DA_EXT), lambda b, h, i: (b, h))
    return pl.pallas_call(
        functools.partial(_flash_body, lam_init, tq, tk, seq_len),
        grid=(batch, DA_HEADS, nq),
        in_specs=[qspec, qspec, kspec, kspec,
                  pl.BlockSpec((4, HEAD_DIM), lambda b, h, i: (0, 0)),
                  pl.BlockSpec((1, DA_V_DIM), lambda b, h, i: (0, 0))],
        out_specs=pl.BlockSpec((tq, DA_V_DIM), lambda b, h, i: (b * nq + i, h)),
        out_shape=jax.ShapeDtypeStruct((T, DA_WIDTH), f32),
        scratch_shapes=[pltpu.VMEM((2, seq_len // tk, tq, tk), f32), pltpu.VMEM((2, tq, DA_EXT), f32)],
        compiler_params=pltpu.CompilerParams(
            dimension_semantics=("arbitrary", "arbitrary", "arbitrary"),
            vmem_limit_bytes=VMEM_LIMIT),
        name="flash_diff_attn",
    )(q0, q1, k, v, lam_vec, subln)


def _out_body(h_ref, yf_ref, yb_ref, bonus_ref, rest_ref, o_ref, p_ref, seg_ref, lng_ref, lnb_ref,
              wout_ref, pproj_ref, pnorm_ref, gw_ref, gb_ref, out_ref):
    seg = seg_ref[...]
    inv_n = 1.0 / HEAD_DIM
    y = yf_ref[...] + yb_ref[...]
    mu = _segsum(y, seg) * inv_n
    yc = y - mu
    var = _segsum(yc * yc, seg) * inv_n
    y_rw = (yc * lax.rsqrt(var + LN_X_EPS) * lng_ref[...] + lnb_ref[...] + bonus_ref[...])
    y_rw = y_rw * _silu(rest_ref[:, 0:512])
    y_da = o_ref[...] * _silu(rest_ref[:, 2048:2560])
    h1 = (h_ref[...] + _dot(y_rw.astype(bf16), wout_ref[0:512, :])
          + _dot(y_da.astype(bf16), wout_ref[512:1024, :]))
    e = _dot(p_ref[...].astype(bf16), pproj_ref[...])
    e = e * lax.rsqrt(jnp.mean(e * e, axis=-1, keepdims=True) + NORM_EPS) * pnorm_ref[...]
    gate = _sigmoid(_dot(h1.astype(bf16), gw_ref[...]) + gb_ref[...])
    out_ref[...] = h1 + gate * e


def _out_stage(h, y_f, y_b, bonus, rest, o, p, prm):
    T = h.shape[0]
    tm = TOKEN_TILE
    row = lambda w: pl.BlockSpec((tm, w), lambda i: (i, 0))
    const = lambda shape: pl.BlockSpec(shape, lambda i: (0,) * len(shape))
    return pl.pallas_call(
        _out_body,
        grid=(T // tm,),
        in_specs=[row(D_MODEL), row(512), row(512), row(512), row(REST_COLS), row(512), row(D_PLE),
                  const((512, 512)), const((1, 512)), const((1, 512)),
                  const((D_MODEL, D_MODEL)), const((D_PLE, D_MODEL)), const((1, D_MODEL)),
                  const((D_MODEL, D_MODEL)), const((1, D_MODEL))],
        out_specs=row(D_MODEL),
        out_shape=jax.ShapeDtypeStruct((T, D_MODEL), f32),
        compiler_params=pltpu.CompilerParams(
            dimension_semantics=("arbitrary",), vmem_limit_bytes=VMEM_LIMIT),
        name="out_stage",
    )(h, y_f, y_b, bonus, rest, o, p, prm["seg"], prm["ln_g"], prm["ln_b"], prm["w_out"],
      prm["ple_proj"], prm["ple_norm"], prm["gate_w"], prm["gate_b"])


def _chunk_triangles(tm):
    t = np.arange(tm)
    same = (t[:, None] // CHUNK) == (t[None, :] // CHUNK)
    low_incl = same & (t[None, :] <= t[:, None])
    up_strict = same & (t[None, :] > t[:, None])
    up_incl = same & (t[None, :] >= t[:, None])
    low_strict = same & (t[None, :] < t[:, None])
    tri = np.stack([np.concatenate([low_incl, up_strict], 0), np.concatenate([up_incl, low_strict], 0)])
    return jnp.asarray(tri.astype(np.float32), dtype=bf16)


def _layer_params(i, norm_pre, w_in, w_out, rw_conv_w, rw_conv_b, rw_w0, rw_w_up, rw_a0, rw_a_up, rw_k_k,
                  rw_k_a, rw_r_k, rw_ln_g, rw_ln_b, da_q_norm, da_k_norm, da_lambda, da_subln, ple_proj,
                  ple_norm, ple_gate_w, ple_gate_b):
    pad = CONV_PAD - (CONV_COLS)
    w = w_in[i]
    w_pad = jnp.concatenate(
        [w[:, :CONV_COLS], jnp.zeros((D_MODEL, pad), f32), w[:, CONV_COLS:]], axis=1).astype(bf16)
    wup = rw_w_up[i]
    zeros = jnp.zeros((W_LORA, RW_WIDTH), f32)
    w_up_bd = jnp.concatenate(
        [jnp.concatenate([wup[0], zeros], 1), jnp.concatenate([zeros, wup[1]], 1)], 0).astype(bf16)
    a_up_pad = jnp.concatenate([rw_a_up[i], jnp.zeros((64, RW_WIDTH), f32)], 0).astype(bf16)
    lane = np.arange(512)
    seg = jnp.asarray((lane[:, None] // HEAD_DIM == lane[None, :] // HEAD_DIM).astype(np.float32), dtype=bf16)
    return dict(
        norm_pre=norm_pre[i].reshape(1, D_MODEL),
        w_in=w_pad,
        conv_w=jnp.pad(rw_conv_w[i], ((0, 0), (0, pad))),
        conv_b=jnp.pad(rw_conv_b[i], (0, pad)).reshape(1, CONV_PAD),
        w0=rw_w0[i].reshape(1, 2 * RW_WIDTH),
        w_up=w_up_bd,
        a0=rw_a0[i].reshape(1, RW_WIDTH),
        a_up=a_up_pad,
        k_k=rw_k_k[i].reshape(1, RW_WIDTH),
        k_a=rw_k_a[i].reshape(1, RW_WIDTH),
        r_k=rw_r_k[i].reshape(1, RW_WIDTH),
        ln_g=rw_ln_g[i].reshape(1, RW_WIDTH),
        ln_b=rw_ln_b[i].reshape(1, RW_WIDTH),
        q_norm=jnp.tile(da_q_norm[i].reshape(1, 2 * HEAD_DIM), (1, DA_HEADS)),
        k_norm=jnp.tile(da_k_norm[i].reshape(1, 2 * HEAD_DIM), (1, DA_HEADS)),
        lam_vec=da_lambda[i],
        subln=da_subln[i].reshape(1, DA_V_DIM),
        w_out=w_out[i].astype(bf16),
        ple_proj=ple_proj[i].astype(bf16),
        ple_norm=ple_norm[i].reshape(1, D_MODEL),
        gate_w=ple_gate_w[i].astype(bf16),
        gate_b=ple_gate_b[i].reshape(1, D_MODEL),
        seg=seg,
        tri=_chunk_triangles(TOKEN_TILE),
    )


def _layer(h, p, lam_init, prm, batch, seq_len):
    conv_in, rest = _inproj(h, prm["norm_pre"], prm["w_in"])
    slab, v_rw, bonus = _rwkv_prep(conv_in, seq_len, prm)
    y_f, y_b = _rwkv_scan(slab, v_rw, batch, seq_len)
    q0, q1, k, v = _attn_prep(rest, seq_len, prm)
    o = _flash(q0, q1, k, v, prm["lam_vec"], prm["subln"], lam_init, batch, seq_len)
    return _out_stage(h, y_f, y_b, bonus, rest, o, p, prm)


def _trunk(x, p, layers):
    batch, seq_len, _ = x.shape
    h = x.reshape(batch * seq_len, D_MODEL)
    for i, prm in enumerate(layers):
        lam_init = 0.8 - 0.6 * math.exp(-0.3 * i)
        h = _layer(h, p[i].reshape(batch * seq_len, D_PLE), lam_init, prm, batch, seq_len)
    return h.reshape(batch, seq_len, D_MODEL)


def kernel(x_prompt, x_sample, p_prompt, p_sample, norm_pre, w_in, w_out, rw_conv_w, rw_conv_b, rw_w0, rw_w_up, rw_a0, rw_a_up, rw_k_k, rw_k_a, rw_r_k, rw_ln_g, rw_ln_b, da_q_norm, da_k_norm, da_lambda, da_subln, ple_proj, ple_norm, ple_gate_w, ple_gate_b):
    depth = norm_pre.shape[0]
    layers = [_layer_params(i, norm_pre, w_in, w_out, rw_conv_w, rw_conv_b, rw_w0, rw_w_up, rw_a0, rw_a_up,
                            rw_k_k, rw_k_a, rw_r_k, rw_ln_g, rw_ln_b, da_q_norm, da_k_norm, da_lambda,
                            da_subln, ple_proj, ple_norm, ple_gate_w, ple_gate_b) for i in range(depth)]
    return (_trunk(x_prompt, p_prompt, layers), _trunk(x_sample, p_sample, layers))
```

```python
import functools
import math

import numpy as np
import jax
import jax.numpy as jnp
from jax import lax
from jax.experimental import pallas as pl
from jax.experimental.pallas import tpu as pltpu

f32 = jnp.float32
bf16 = jnp.bfloat16

D_MODEL = 1024
D_PLE = 256
RW_WIDTH = 512
HEAD_DIM = 64
W_LORA = 64
CONV_COLS = 3 * RW_WIDTH + 2 * W_LORA + 64
CONV_PAD = 1792
DA_HEADS = 4
DA_V_DIM = 128
DA_WIDTH = 512
DA_EXT = 256
REST_COLS = 5 * 512
NORM_EPS = 1e-6
LN_X_EPS = 64e-5
KK_EPS = 1e-12
ALIBI_MAX_BIAS = 8.0
DECAY_SCALE = math.exp(-0.5)
LOG2_E = math.log2(math.e)
EXP2_UNDERFLOW = 160.0
MAX_REACH = 1 << 24
POS_SPLIT = 128

CHUNK = 64
GROUP = 256
HEADS_PER_GROUP = GROUP // HEAD_DIM
STACK = HEADS_PER_GROUP * CHUNK
TOKEN_TILE = 256
FLASH_TQ = 256
FLASH_TK = 512
VMEM_LIMIT = 48 * 1024 * 1024


def _dot(a, b):
    return jnp.dot(a, b, preferred_element_type=f32)


def _dot_nt(a, b):
    return lax.dot_general(a, b, (((1,), (1,)), ((), ())), preferred_element_type=f32)


def _split2(x):
    hi = x.astype(bf16)
    lo = (x - hi.astype(f32)).astype(bf16)
    return hi, lo


def _split3(x):
    hi = x.astype(bf16)
    r1 = x - hi.astype(f32)
    mid = r1.astype(bf16)
    lo = (r1 - mid.astype(f32)).astype(bf16)
    return hi, mid, lo


def _dot_exact01(m01, x, parts):
    ps = _split3(x) if parts == 3 else _split2(x)
    acc = _dot(m01, ps[0])
    for p in ps[1:]:
        acc = acc + _dot(m01, p)
    return acc


def _segsum(x, seg):
    hi, lo = _split2(x)
    return _dot(hi, seg) + _dot(lo, seg)


def _sigmoid(x):
    return 1.0 / (1.0 + jnp.exp(-x))


def _silu(x):
    return x * _sigmoid(x)


def _inproj_body(x_ref, g_ref, w_ref, conv_ref, rest_ref):
    x = x_ref[...]
    ms = jnp.mean(x * x, axis=-1, keepdims=True)
    u = (x * lax.rsqrt(ms + NORM_EPS) * g_ref[...]).astype(bf16)
    for c0 in range(0, CONV_PAD, 256):
        conv_ref[:, c0:c0 + 256] = _dot(u, w_ref[:, c0:c0 + 256])
    for c0 in range(0, REST_COLS, 256):
        rest_ref[:, c0:c0 + 256] = _dot(u, w_ref[:, CONV_PAD + c0:CONV_PAD + c0 + 256])


def _inproj(h, norm_g, w_pad):
    T = h.shape[0]
    tm = TOKEN_TILE
    return pl.pallas_call(
        _inproj_body,
        grid=(T // tm,),
        in_specs=[
            pl.BlockSpec((tm, D_MODEL), lambda i: (i, 0)),
            pl.BlockSpec((1, D_MODEL), lambda i: (0, 0)),
            pl.BlockSpec((D_MODEL, CONV_PAD + REST_COLS), lambda i: (0, 0)),
        ],
        out_specs=[
            pl.BlockSpec((tm, CONV_PAD), lambda i: (i, 0)),
            pl.BlockSpec((tm, REST_COLS), lambda i: (i, 0)),
        ],
        out_shape=[
            jax.ShapeDtypeStruct((T, CONV_PAD), f32),
            jax.ShapeDtypeStruct((T, REST_COLS), f32),
        ],
        compiler_params=pltpu.CompilerParams(
            dimension_semantics=("arbitrary",), vmem_limit_bytes=VMEM_LIMIT),
        name="inproj",
    )(h, norm_g, w_pad)


def _rwkv_prep_body(tiles_per_seq, main_ref, prev_ref, next_ref, cw_ref, cb_ref, w0_ref, wup_ref,
                    a0_ref, aup_ref, kk_ref, ka_ref, rk_ref, seg_ref, tri_ref,
                    slab_ref, v_ref, bonus_ref):
    tm = main_ref.shape[0]
    i = pl.program_id(0)
    local = i % tiles_per_seq
    is_first = local == 0
    is_last = local == tiles_per_seq - 1

    xm = main_ref[...]
    row = lax.broadcasted_iota(jnp.int32, (tm, 1), 0)
    prev_row = jnp.where(is_first, 0.0, prev_ref[7:8, :])
    next_row = jnp.where(is_last, 0.0, next_ref[0:1, :])
    xp = jnp.where(row == 0, prev_row, pltpu.roll(xm, 1, 0))
    xn = jnp.where(row == tm - 1, next_row, pltpu.roll(xm, tm - 1, 0))
    cw = cw_ref[...]
    c = cb_ref[...] + xp * cw[0:1] + xm * cw[1:2] + xn * cw[2:3]

    r = c[:, 0:512]
    k = c[:, 512:1024]
    v = c[:, 1024:1536]
    zw = c[:, 1536:1664]
    za = c[:, 1664:1792]

    wl = w0_ref[...] + _dot(jnp.tanh(zw).astype(bf16), wup_ref[...])
    logw = -DECAY_SCALE * _sigmoid(wl)
    a = _sigmoid(a0_ref[...] + _dot(za.astype(bf16), aup_ref[...]))
    seg = seg_ref[...]
    kk = k * kk_ref[...]
    kk = kk * lax.rsqrt(_segsum(kk * kk, seg) + KK_EPS)
    kmod = k * (1.0 + (a - 1.0) * ka_ref[...])
    kka = kk * a
    bonus_ref[...] = _segsum(r * kmod * rk_ref[...], seg) * v
    v_ref[...] = v

    for d in range(2):
        lw = logw[:, d * 512:(d + 1) * 512]
        cums = _dot_exact01(tri_ref[d], lw, 3)
        cum_in = cums[:tm]
        rev_ex = cums[tm:]
        e_pos = jnp.exp(cum_in)
        e_neg = jnp.exp(-cum_in)
        e_rev = jnp.exp(rev_ex)
        slab_ref[d, 0] = r * e_pos
        slab_ref[d, 1] = kmod * e_neg
        slab_ref[d, 2] = -kk * jnp.exp(cum_in - lw)
        slab_ref[d, 3] = kka * e_neg
        slab_ref[d, 4] = kka * e_rev
        slab_ref[d, 5] = kmod * e_rev
        slab_ref[d, 6] = jnp.exp(cum_in + rev_ex)


def _rwkv_prep(conv_in, seq_len, prm):
    T = conv_in.shape[0]
    tm = TOKEN_TILE
    tiles_per_seq = seq_len // tm
    n8 = T // 8
    const = lambda shape: pl.BlockSpec(shape, lambda i: (0,) * len(shape))
    return pl.pallas_call(
        functools.partial(_rwkv_prep_body, tiles_per_seq),
        grid=(T // tm,),
        in_specs=[
            pl.BlockSpec((tm, CONV_PAD), lambda i: (i, 0)),
            pl.BlockSpec((8, CONV_PAD), lambda i: (jnp.maximum(i * (tm // 8) - 1, 0), 0)),
            pl.BlockSpec((8, CONV_PAD), lambda i: (jnp.minimum((i + 1) * (tm // 8), n8 - 1), 0)),
            const((3, CONV_PAD)), const((1, CONV_PAD)), const((1, 1024)), const((128, 1024)),
            const((1, 512)), const((128, 512)), const((1, 512)), const((1, 512)), const((1, 512)),
            const((512, 512)), const((2, 2 * tm, tm)),
        ],
        out_specs=[
            pl.BlockSpec((2, 7, tm, 512), lambda i: (0, 0, i, 0)),
            pl.BlockSpec((tm, 512), lambda i: (i, 0)),
            pl.BlockSpec((tm, 512), lambda i: (i, 0)),
        ],
        out_shape=[
            jax.ShapeDtypeStruct((2, 7, T, 512), f32),
            jax.ShapeDtypeStruct((T, 512), f32),
            jax.ShapeDtypeStruct((T, 512), f32),
        ],
        compiler_params=pltpu.CompilerParams(
            dimension_semantics=("arbitrary",), vmem_limit_bytes=VMEM_LIMIT),
        name="rwkv_prep",
    )(conv_in, conv_in, conv_in, prm["conv_w"], prm["conv_b"], prm["w0"], prm["w_up"],
      prm["a0"], prm["a_up"], prm["k_k"], prm["k_a"], prm["r_k"], prm["seg"], prm["tri"])


def _mm(a, b):
    return _dot(a.astype(bf16), b.astype(bf16))


def _stack_heads(x, head_masks):
    return jnp.concatenate([x * hm for hm in head_masks], axis=0)


def _rwkv_scan_body(slab_f_ref, slab_b_ref, v_f_ref, v_b_ref, y_f_ref, y_b_ref, z_ref):
    L = CHUNK
    n_groups = RW_WIDTH // GROUP

    @pl.when(pl.program_id(1) == 0)
    def _():
        z_ref[...] = jnp.zeros_like(z_ref)

    rs = lax.broadcasted_iota(jnp.int32, (STACK, STACK), 0)
    cs = lax.broadcasted_iota(jnp.int32, (STACK, STACK), 1)
    eye = rs == cs
    rt_idx = rs % L
    ct_idx = cs % L
    lane = lax.broadcasted_iota(jnp.int32, (1, GROUP), 1) // HEAD_DIM
    head_masks = [(lane == h).astype(f32) for h in range(HEADS_PER_GROUP)]
    strict = [(ct_idx < rt_idx).astype(f32), (ct_idx > rt_idx).astype(f32)]
    incl = [(ct_idx <= rt_idx).astype(f32), (ct_idx >= rt_idx).astype(f32)]

    chains = []
    for d, (slab_ref, v_ref) in enumerate(((slab_f_ref, v_f_ref), (slab_b_ref, v_b_ref))):
        for g in range(n_groups):
            lanes = slice(g * GROUP, (g + 1) * GROUP)
            chains.append(dict(
                d=d, g=g, lanes=lanes,
                r=_stack_heads(slab_ref[0, :, lanes], head_masks),
                k=_stack_heads(slab_ref[1, :, lanes], head_masks),
                a=_stack_heads(slab_ref[2, :, lanes], head_masks),
                b=_stack_heads(slab_ref[3, :, lanes], head_masks),
                bp=_stack_heads(slab_ref[4, :, lanes], head_masks),
                kp=_stack_heads(slab_ref[5, :, lanes], head_masks),
                pdiag=slab_ref[6, 0:1, lanes],
                v=_stack_heads(v_ref[:, lanes], head_masks),
            ))

    for ch in chains:
        d = ch["d"]
        gram = _dot_nt(jnp.concatenate([ch["a"], ch["r"]], axis=0).astype(bf16),
                       jnp.concatenate([ch["b"], ch["k"]], axis=0).astype(bf16))
        ch["m_ab"] = gram[:STACK, :STACK] * strict[d]
        ch["m_ak"] = gram[:STACK, STACK:] * strict[d]
        ch["m_rb"] = gram[STACK:, :STACK] * incl[d]
        ch["m_rk"] = gram[STACK:, STACK:] * incl[d]
    for ch in chains:
        ch["t"] = jnp.where(eye, 1.0, 0.0) + ch["m_ab"]
        ch["q"] = _mm(ch["m_ab"], ch["m_ab"])
    for _ in range(4):
        for ch in chains:
            p = _mm(jnp.concatenate([ch["t"], ch["q"]], axis=0), ch["q"])
            ch["t"] = ch["t"] + p[:STACK]
            ch["q"] = p[STACK:]
    for ch in chains:
        ch["t"] = ch["t"] + _mm(ch["t"], ch["q"])
    for ch in chains:
        ch["w1"] = _mm(ch["m_ak"], ch["v"])
    for ch in chains:
        ch["au"] = _mm(ch["t"], jnp.concatenate([ch["a"], ch["w1"]], axis=1))
    for ch in chains:
        o1 = _mm(ch["m_rb"], ch["au"])
        ch["r_hat"] = ch["r"] + o1[:, :GROUP]
        ch["y_hat"] = o1[:, GROUP:] + _mm(ch["m_rk"], ch["v"])
    for ch in chains:
        bp_t = ch["bp"].T
        kp_t = ch["kp"].T
        a_hat = ch["au"][:, :GROUP]
        u_hat = ch["au"][:, GROUP:]
        ch["phi_t"] = _mm(bp_t, a_hat) + jnp.where(eye, ch["pdiag"], 0.0)
        ch["psi_t"] = _mm(jnp.concatenate([bp_t, kp_t], axis=1), jnp.concatenate([u_hat, ch["v"]], axis=0))
    y_refs = (y_f_ref, y_b_ref)
    for ch in chains:
        z = z_ref[ch["d"], ch["g"]].astype(bf16)
        y_s = _dot(ch["r_hat"].astype(bf16), z) + ch["y_hat"]
        z_ref[ch["d"], ch["g"]] = _dot(ch["phi_t"].astype(bf16), z) + ch["psi_t"]
        y = y_s[0:L]
        for h in range(1, HEADS_PER_GROUP):
            y = y + y_s[h * L:(h + 1) * L]
        y_refs[ch["d"]][:, ch["lanes"]] = y


def _rwkv_scan(slab, v, batch, seq_len):
    T = v.shape[0]
    L = CHUNK
    nc = seq_len // L
    fwd = lambda b, c: b * nc + c
    bwd = lambda b, c: b * nc + (nc - 1 - c)
    return pl.pallas_call(
        _rwkv_scan_body,
        grid=(batch, nc),
        in_specs=[
            pl.BlockSpec((None, 7, L, 512), lambda b, c: (0, 0, fwd(b, c), 0)),
            pl.BlockSpec((None, 7, L, 512), lambda b, c: (1, 0, bwd(b, c), 0)),
            pl.BlockSpec((L, 512), lambda b, c: (fwd(b, c), 0)),
            pl.BlockSpec((L, 512), lambda b, c: (bwd(b, c), 0)),
        ],
        out_specs=[
            pl.BlockSpec((L, 512), lambda b, c: (fwd(b, c), 0)),
            pl.BlockSpec((L, 512), lambda b, c: (bwd(b, c), 0)),
        ],
        out_shape=[jax.ShapeDtypeStruct((T, 512), f32), jax.ShapeDtypeStruct((T, 512), f32)],
        scratch_shapes=[pltpu.VMEM((2, RW_WIDTH // GROUP, GROUP, GROUP), f32)],
        compiler_params=pltpu.CompilerParams(
            dimension_semantics=("arbitrary", "arbitrary"), vmem_limit_bytes=VMEM_LIMIT),
        name="rwkv_scan",
    )(slab, slab, v, v)


def _attn_prep_body(tiles_per_seq, rest_ref, qn_ref, kn_ref, seg_ref, q0_ref, q1_ref, k_ref, v_ref):
    tm = rest_ref.shape[0]
    seg = seg_ref[...]
    q = rest_ref[:, 512:1024]
    k = rest_ref[:, 1024:1536]
    v = rest_ref[:, 1536:2048]
    inv_d = 1.0 / HEAD_DIM
    qms = _segsum(q * q, seg) * inv_d
    kms = _segsum(k * k, seg) * inv_d
    qn = q * lax.rsqrt(qms + NORM_EPS) * qn_ref[...] * (HEAD_DIM ** -0.5)
    kn = k * lax.rsqrt(kms + NORM_EPS) * kn_ref[...]
    lane = lax.broadcasted_iota(jnp.int32, (1, DA_WIDTH), 1)
    first_map = (lane // HEAD_DIM) % 2 == 0
    q0 = jnp.where(first_map, qn, 0.0)
    q1 = jnp.where(first_map, 0.0, qn)

    pos = (pl.program_id(0) % tiles_per_seq) * tm + lax.broadcasted_iota(jnp.int32, (tm, 1), 0)
    hi = (pos // POS_SPLIT).astype(f32)
    lo = (pos % POS_SPLIT).astype(f32)
    al = lax.broadcasted_iota(jnp.int32, (1, DA_V_DIM), 1)
    ones_col = jnp.where(al == 0, 1.0, 0.0) + jnp.zeros((tm, DA_V_DIM), f32)
    for h in range(DA_HEADS):
        slope = 2.0 ** (-(ALIBI_MAX_BIAS / DA_HEADS) * (h + 1))
        q_aug = jnp.where(al == 0, -slope * POS_SPLIT * hi,
                          jnp.where(al == 1, -slope * lo, jnp.where(al < 4, 1.0, 0.0)))
        k_aug = jnp.where(al < 2, 1.0,
                          jnp.where(al == 2, slope * POS_SPLIT * hi, jnp.where(al == 3, slope * lo, 0.0)))
        src = slice(h * DA_V_DIM, (h + 1) * DA_V_DIM)
        main = slice(h * DA_EXT, h * DA_EXT + DA_V_DIM)
        aug = slice(h * DA_EXT + DA_V_DIM, (h + 1) * DA_EXT)
        q0_ref[:, main] = q0[:, src].astype(bf16)
        q1_ref[:, main] = q1[:, src].astype(bf16)
        k_ref[:, main] = kn[:, src].astype(bf16)
        v_ref[:, main] = v[:, src].astype(bf16)
        q0_ref[:, aug] = q_aug.astype(bf16)
        q1_ref[:, aug] = q_aug.astype(bf16)
        k_ref[:, aug] = k_aug.astype(bf16)
        v_ref[:, aug] = ones_col.astype(bf16)


def _attn_prep(rest, seq_len, prm):
    T = rest.shape[0]
    tm = TOKEN_TILE
    width = DA_HEADS * DA_EXT
    out = jax.ShapeDtypeStruct((T, width), bf16)
    spec = pl.BlockSpec((tm, width), lambda i: (i, 0))
    return pl.pallas_call(
        functools.partial(_attn_prep_body, seq_len // tm),
        grid=(T // tm,),
        in_specs=[
            pl.BlockSpec((tm, REST_COLS), lambda i: (i, 0)),
            pl.BlockSpec((1, DA_WIDTH), lambda i: (0, 0)),
            pl.BlockSpec((1, DA_WIDTH), lambda i: (0, 0)),
            pl.BlockSpec((512, 512), lambda i: (0, 0)),
        ],
        out_specs=[spec, spec, spec, spec],
        out_shape=[out, out, out, out],
        compiler_params=pltpu.CompilerParams(
            dimension_semantics=("arbitrary",), vmem_limit_bytes=VMEM_LIMIT),
        name="attn_prep",
    )(rest, prm["q_norm"], prm["k_norm"], prm["seg"])


def _flash_body(lam_init, tq, tk, seq_len, reach_ref, q0_ref, q1_ref, k_ref, v_ref, lam_ref, subln_ref,
                o_ref, s_ref, acc_ref):
    h = pl.program_id(1)
    qi = pl.program_id(2)
    nk = seq_len // tk
    lane_blocks = tk // DA_V_DIM

    kd = (qi * tq) // tk
    q_main = (q0_ref[:, :DA_V_DIM], q1_ref[:, :DA_V_DIM])
    q_aug = q0_ref[:, DA_V_DIM:]

    def tile_start(kt):
        return kt * tk if isinstance(kt, int) else pl.multiple_of(kt * tk, tk)

    def scores(kt, bias, running):
        k = k_ref[pl.ds(tile_start(kt), tk), :]
        if bias is None:
            sign = jnp.where(kt < kd, 1.0, jnp.where(kt > kd, -1.0, 0.0)).astype(f32)
            aug = (q_aug.astype(f32) * sign).astype(bf16)
        else:
            aug = jnp.zeros_like(q_aug)
        out = []
        for mi in range(2):
            s = _dot_nt(jnp.concatenate([q_main[mi], aug], axis=1), k)
            if bias is not None:
                s = s - bias
            s = s * LOG2_E
            s_ref[mi, kt] = s
            mx = running[mi]
            for c in range(lane_blocks):
                mx = jnp.maximum(mx, s[:, c * DA_V_DIM:(c + 1) * DA_V_DIM])
            out.append(mx if bias is not None else jnp.where(kt == kd, running[mi], mx))
        return tuple(out)

    span = 2 * tk
    reach = reach_ref[h]
    first_pair = jnp.maximum(qi * tq - reach, 0) // span
    n_pairs = jnp.minimum((qi * tq + tq - 1 + reach) // span + 1, nk // 2) - first_pair

    def score_pair(j, running):
        return scores(2 * j + 1, None, scores(2 * j, None, running))

    running = (jnp.full((tq, DA_V_DIM), -jnp.inf, f32),) * 2
    running = lax.fori_loop(
        0, n_pairs // 2,
        lambda i, r: score_pair(first_pair + 2 * i + 1, score_pair(first_pair + 2 * i, r)), running)
    running = lax.fori_loop(
        0, n_pairs % 2, lambda i, r: score_pair(first_pair + n_pairs - 1, r), running)
    if nk % 2:
        running = scores(nk - 1, None, running)
    slope = jnp.exp2(jnp.full((1, 1), -(ALIBI_MAX_BIAS / DA_HEADS), f32) * (h + 1).astype(f32))
    rel = (lax.broadcasted_iota(jnp.int32, (tq, tk), 0) - lax.broadcasted_iota(jnp.int32, (tq, tk), 1)
           + (qi * tq - kd * tk))
    running = scores(kd, slope * jnp.abs(rel).astype(f32), running)
    row_max = [jnp.broadcast_to(jnp.max(mx, axis=-1, keepdims=True), (tq, DA_V_DIM)) for mx in running]

    acc_ref[...] = jnp.zeros_like(acc_ref)

    def weighted_values(mi, first, count):
        v = v_ref[pl.ds(tile_start(first), count * tk), :]
        ps = [jnp.exp2(s_ref[mi, first + j, :, c * DA_V_DIM:(c + 1) * DA_V_DIM] - row_max[mi]).astype(bf16)
              for j in range(count) for c in range(lane_blocks)]
        return _dot(jnp.concatenate(ps, axis=1), v)

    def two_pairs(i, carry):
        j = first_pair + 2 * i
        for mi in range(2):
            acc_ref[mi] += weighted_values(mi, 2 * j, 2) + weighted_values(mi, 2 * j + 2, 2)
        return carry

    def last_pair(i, carry):
        for mi in range(2):
            acc_ref[mi] += weighted_values(mi, 2 * (first_pair + n_pairs - 1), 2)
        return carry

    lax.fori_loop(0, n_pairs // 2, two_pairs, 0)
    lax.fori_loop(0, n_pairs % 2, last_pair, 0)
    if nk % 2:
        for mi in range(2):
            acc_ref[mi] += weighted_values(mi, nk - 1, 1)

    lv = lam_ref[...]
    lam = (jnp.exp(jnp.sum(lv[0:1] * lv[1:2], axis=-1, keepdims=True))
           - jnp.exp(jnp.sum(lv[2:3] * lv[3:4], axis=-1, keepdims=True)) + lam_init)
    acc0 = acc_ref[0]
    acc1 = acc_ref[1]
    o = (acc0[:, :DA_V_DIM] / acc0[:, DA_V_DIM:DA_V_DIM + 1]
         - lam * (acc1[:, :DA_V_DIM] / acc1[:, DA_V_DIM:DA_V_DIM + 1]))
    ms = jnp.mean(o * o, axis=-1, keepdims=True)
    o_ref[...] = o * lax.rsqrt(ms + NORM_EPS) * subln_ref[...] * (1.0 - lam_init)


def _alibi_reach(q_norm, k_norm):
    gq = jnp.max(jnp.abs(q_norm))
    gk = jnp.max(jnp.abs(k_norm))
    slopes = 2.0 ** (-(ALIBI_MAX_BIAS / DA_HEADS) * jnp.arange(1, DA_HEADS + 1, dtype=f32))
    dist = (1.05 * 16.0 * gq * gk + EXP2_UNDERFLOW * math.log(2.0)) / slopes
    return jnp.minimum(jnp.ceil(dist), float(MAX_REACH)).astype(jnp.int32)


def _flash(q0, q1, k, v, reach, lam_vec, subln, lam_init, batch, seq_len):
    T = q0.shape[0]
    tq = min(FLASH_TQ, seq_len)
    tk = min(FLASH_TK, seq_len)
    nq = seq_len // tq
    qspec = pl.BlockSpec((tq, DA_EXT), lambda b, h, i, reach_ref: (b * nq + i, h))
    kspec = pl.BlockSpec((seq_len, DA_EXT), lambda b, h, i, reach_ref: (b, h))
    return pl.pallas_call(
        functools.partial(_flash_body, lam_init, tq, tk, seq_len),
        grid_spec=pltpu.PrefetchScalarGridSpec(
            num_scalar_prefetch=1,
            grid=(batch, DA_HEADS, nq),
            in_specs=[qspec, qspec, kspec, kspec,
                      pl.BlockSpec((4, HEAD_DIM), lambda b, h, i, reach_ref: (0, 0)),
                      pl.BlockSpec((1, DA_V_DIM), lambda b, h, i, reach_ref: (0, 0))],
            out_specs=pl.BlockSpec((tq, DA_V_DIM), lambda b, h, i, reach_ref: (b * nq + i, h)),
            scratch_shapes=[pltpu.VMEM((2, seq_len // tk, tq, tk), f32), pltpu.VMEM((2, tq, DA_EXT), f32)],
        ),
        out_shape=jax.ShapeDtypeStruct((T, DA_WIDTH), f32),
        compiler_params=pltpu.CompilerParams(
            dimension_semantics=("arbitrary", "arbitrary", "arbitrary"),
            vmem_limit_bytes=VMEM_LIMIT),
        name="flash_diff_attn",
    )(reach, q0, q1, k, v, lam_vec, subln)


def _out_body(h_ref, yf_ref, yb_ref, bonus_ref, rest_ref, o_ref, p_ref, seg_ref, lng_ref, lnb_ref,
              wout_ref, pproj_ref, pnorm_ref, gw_ref, gb_ref, out_ref):
    seg = seg_ref[...]
    inv_n = 1.0 / HEAD_DIM
    y = yf_ref[...] + yb_ref[...]
    mu = _segsum(y, seg) * inv_n
    yc = y - mu
    var = _segsum(yc * yc, seg) * inv_n
    y_rw = (yc * lax.rsqrt(var + LN_X_EPS) * lng_ref[...] + lnb_ref[...] + bonus_ref[...])
    y_rw = y_rw * _silu(rest_ref[:, 0:512])
    y_da = o_ref[...] * _silu(rest_ref[:, 2048:2560])
    h1 = (h_ref[...] + _dot(y_rw.astype(bf16), wout_ref[0:512, :])
          + _dot(y_da.astype(bf16), wout_ref[512:1024, :]))
    e = _dot(p_ref[...].astype(bf16), pproj_ref[...])
    e = e * lax.rsqrt(jnp.mean(e * e, axis=-1, keepdims=True) + NORM_EPS) * pnorm_ref[...]
    gate = _sigmoid(_dot(h1.astype(bf16), gw_ref[...]) + gb_ref[...])
    out_ref[...] = h1 + gate * e


def _out_stage(h, y_f, y_b, bonus, rest, o, p, prm):
    T = h.shape[0]
    tm = TOKEN_TILE
    row = lambda w: pl.BlockSpec((tm, w), lambda i: (i, 0))
    const = lambda shape: pl.BlockSpec(shape, lambda i: (0,) * len(shape))
    return pl.pallas_call(
        _out_body,
        grid=(T // tm,),
        in_specs=[row(D_MODEL), row(512), row(512), row(512), row(REST_COLS), row(512), row(D_PLE),
                  const((512, 512)), const((1, 512)), const((1, 512)),
                  const((D_MODEL, D_MODEL)), const((D_PLE, D_MODEL)), const((1, D_MODEL)),
                  const((D_MODEL, D_MODEL)), const((1, D_MODEL))],
        out_specs=row(D_MODEL),
        out_shape=jax.ShapeDtypeStruct((T, D_MODEL), f32),
        compiler_params=pltpu.CompilerParams(
            dimension_semantics=("arbitrary",), vmem_limit_bytes=VMEM_LIMIT),
        name="out_stage",
    )(h, y_f, y_b, bonus, rest, o, p, prm["seg"], prm["ln_g"], prm["ln_b"], prm["w_out"],
      prm["ple_proj"], prm["ple_norm"], prm["gate_w"], prm["gate_b"])


def _chunk_triangles(tm):
    t = np.arange(tm)
    same = (t[:, None] // CHUNK) == (t[None, :] // CHUNK)
    low_incl = same & (t[None, :] <= t[:, None])
    up_strict = same & (t[None, :] > t[:, None])
    up_incl = same & (t[None, :] >= t[:, None])
    low_strict = same & (t[None, :] < t[:, None])
    tri = np.stack([np.concatenate([low_incl, up_strict], 0), np.concatenate([up_incl, low_strict], 0)])
    return jnp.asarray(tri.astype(np.float32), dtype=bf16)


def _layer_params(i, norm_pre, w_in, w_out, rw_conv_w, rw_conv_b, rw_w0, rw_w_up, rw_a0, rw_a_up, rw_k_k,
                  rw_k_a, rw_r_k, rw_ln_g, rw_ln_b, da_q_norm, da_k_norm, da_lambda, da_subln, ple_proj,
                  ple_norm, ple_gate_w, ple_gate_b):
    pad = CONV_PAD - (CONV_COLS)
    w = w_in[i]
    w_pad = jnp.concatenate(
        [w[:, :CONV_COLS], jnp.zeros((D_MODEL, pad), f32), w[:, CONV_COLS:]], axis=1).astype(bf16)
    wup = rw_w_up[i]
    zeros = jnp.zeros((W_LORA, RW_WIDTH), f32)
    w_up_bd = jnp.concatenate(
        [jnp.concatenate([wup[0], zeros], 1), jnp.concatenate([zeros, wup[1]], 1)], 0).astype(bf16)
    a_up_pad = jnp.concatenate([rw_a_up[i], jnp.zeros((64, RW_WIDTH), f32)], 0).astype(bf16)
    lane = np.arange(512)
    seg = jnp.asarray((lane[:, None] // HEAD_DIM == lane[None, :] // HEAD_DIM).astype(np.float32), dtype=bf16)
    return dict(
        norm_pre=norm_pre[i].reshape(1, D_MODEL),
        w_in=w_pad,
        conv_w=jnp.pad(rw_conv_w[i], ((0, 0), (0, pad))),
        conv_b=jnp.pad(rw_conv_b[i], (0, pad)).reshape(1, CONV_PAD),
        w0=rw_w0[i].reshape(1, 2 * RW_WIDTH),
        w_up=w_up_bd,
        a0=rw_a0[i].reshape(1, RW_WIDTH),
        a_up=a_up_pad,
        k_k=rw_k_k[i].reshape(1, RW_WIDTH),
        k_a=rw_k_a[i].reshape(1, RW_WIDTH),
        r_k=rw_r_k[i].reshape(1, RW_WIDTH),
        ln_g=rw_ln_g[i].reshape(1, RW_WIDTH),
        ln_b=rw_ln_b[i].reshape(1, RW_WIDTH),
        q_norm=jnp.tile(da_q_norm[i].reshape(1, 2 * HEAD_DIM), (1, DA_HEADS)),
        k_norm=jnp.tile(da_k_norm[i].reshape(1, 2 * HEAD_DIM), (1, DA_HEADS)),
        reach=_alibi_reach(da_q_norm[i], da_k_norm[i]),
        lam_vec=da_lambda[i],
        subln=da_subln[i].reshape(1, DA_V_DIM),
        w_out=w_out[i].astype(bf16),
        ple_proj=ple_proj[i].astype(bf16),
        ple_norm=ple_norm[i].reshape(1, D_MODEL),
        gate_w=ple_gate_w[i].astype(bf16),
        gate_b=ple_gate_b[i].reshape(1, D_MODEL),
        seg=seg,
        tri=_chunk_triangles(TOKEN_TILE),
    )


def _layer(h, p, lam_init, prm, batch, seq_len):
    conv_in, rest = _inproj(h, prm["norm_pre"], prm["w_in"])
    slab, v_rw, bonus = _rwkv_prep(conv_in, seq_len, prm)
    y_f, y_b = _rwkv_scan(slab, v_rw, batch, seq_len)
    q0, q1, k, v = _attn_prep(rest, seq_len, prm)
    o = _flash(q0, q1, k, v, prm["reach"], prm["lam_vec"], prm["subln"], lam_init, batch, seq_len)
    return _out_stage(h, y_f, y_b, bonus, rest, o, p, prm)


def _trunk(x, p, layers):
    batch, seq_len, _ = x.shape
    h = x.reshape(batch * seq_len, D_MODEL)
    for i, prm in enumerate(layers):
        lam_init = 0.8 - 0.6 * math.exp(-0.3 * i)
        h = _layer(h, p[i].reshape(batch * seq_len, D_PLE), lam_init, prm, batch, seq_len)
    return h.reshape(batch, seq_len, D_MODEL)


def kernel(x_prompt, x_sample, p_prompt, p_sample, norm_pre, w_in, w_out, rw_conv_w, rw_conv_b, rw_w0, rw_w_up, rw_a0, rw_a_up, rw_k_k, rw_k_a, rw_r_k, rw_ln_g, rw_ln_b, da_q_norm, da_k_norm, da_lambda, da_subln, ple_proj, ple_norm, ple_gate_w, ple_gate_b):
    depth = norm_pre.shape[0]
    layers = [_layer_params(i, norm_pre, w_in, w_out, rw_conv_w, rw_conv_b, rw_w0, rw_w_up, rw_a0, rw_a_up,
                            rw_k_k, rw_k_a, rw_r_k, rw_ln_g, rw_ln_b, da_q_norm, da_k_norm, da_lambda,
                            da_subln, ple_proj, ple_norm, ple_gate_w, ple_gate_b) for i in range(depth)]
    return (_trunk(x_prompt, p_prompt, layers), _trunk(x_sample, p_sample, layers))
```

```python
import functools
import math

import numpy as np
import jax
import jax.numpy as jnp
from jax import lax
from jax.experimental import pallas as pl
from jax.experimental.pallas import tpu as pltpu

f32 = jnp.float32
bf16 = jnp.bfloat16

D_MODEL = 1024
D_PLE = 256
RW_WIDTH = 512
HEAD_DIM = 64
W_LORA = 64
CONV_COLS = 3 * RW_WIDTH + 2 * W_LORA + 64
CONV_PAD = 1792
DA_HEADS = 4
DA_V_DIM = 128
DA_WIDTH = 512
DA_EXT = 256
REST_COLS = 5 * 512
NORM_EPS = 1e-6
LN_X_EPS = 64e-5
KK_EPS = 1e-12
ALIBI_MAX_BIAS = 8.0
DECAY_SCALE = math.exp(-0.5)
LOG2_E = math.log2(math.e)
EXP2_UNDERFLOW = 160.0
MAX_REACH = 1 << 24
POS_SPLIT = 128

CHUNK = 64
CHUNKS_PER_STEP = 4
GROUP = 256
HEADS_PER_GROUP = GROUP // HEAD_DIM
STACK = HEADS_PER_GROUP * CHUNK
TOKEN_TILE = 256
FLASH_TQ = 256
FLASH_TK = 512
VMEM_LIMIT = 48 * 1024 * 1024


def _dot(a, b):
    return jnp.dot(a, b, preferred_element_type=f32)


def _dot_nt(a, b):
    return lax.dot_general(a, b, (((1,), (1,)), ((), ())), preferred_element_type=f32)


def _split2(x):
    hi = x.astype(bf16)
    lo = (x - hi.astype(f32)).astype(bf16)
    return hi, lo


def _split3(x):
    hi = x.astype(bf16)
    r1 = x - hi.astype(f32)
    mid = r1.astype(bf16)
    lo = (r1 - mid.astype(f32)).astype(bf16)
    return hi, mid, lo


def _dot_exact01(m01, x, parts):
    ps = _split3(x) if parts == 3 else _split2(x)
    acc = _dot(m01, ps[0])
    for p in ps[1:]:
        acc = acc + _dot(m01, p)
    return acc


def _segsum(x, seg):
    hi, lo = _split2(x)
    return _dot(hi, seg) + _dot(lo, seg)


def _sigmoid(x):
    return 1.0 / (1.0 + jnp.exp(-x))


def _silu(x):
    return x * _sigmoid(x)


def _inproj_body(x_ref, g_ref, w_ref, conv_ref, rest_ref):
    x = x_ref[...]
    ms = jnp.mean(x * x, axis=-1, keepdims=True)
    u = (x * lax.rsqrt(ms + NORM_EPS) * g_ref[...]).astype(bf16)
    for c0 in range(0, CONV_PAD, 256):
        conv_ref[:, c0:c0 + 256] = _dot(u, w_ref[:, c0:c0 + 256])
    for c0 in range(0, REST_COLS, 256):
        rest_ref[:, c0:c0 + 256] = _dot(u, w_ref[:, CONV_PAD + c0:CONV_PAD + c0 + 256])


def _inproj(h, norm_g, w_pad):
    T = h.shape[0]
    tm = TOKEN_TILE
    return pl.pallas_call(
        _inproj_body,
        grid=(T // tm,),
        in_specs=[
            pl.BlockSpec((tm, D_MODEL), lambda i: (i, 0)),
            pl.BlockSpec((1, D_MODEL), lambda i: (0, 0)),
            pl.BlockSpec((D_MODEL, CONV_PAD + REST_COLS), lambda i: (0, 0)),
        ],
        out_specs=[
            pl.BlockSpec((tm, CONV_PAD), lambda i: (i, 0)),
            pl.BlockSpec((tm, REST_COLS), lambda i: (i, 0)),
        ],
        out_shape=[
            jax.ShapeDtypeStruct((T, CONV_PAD), f32),
            jax.ShapeDtypeStruct((T, REST_COLS), f32),
        ],
        compiler_params=pltpu.CompilerParams(
            dimension_semantics=("arbitrary",), vmem_limit_bytes=VMEM_LIMIT),
        name="inproj",
    )(h, norm_g, w_pad)


def _rwkv_prep_body(tiles_per_seq, main_ref, prev_ref, next_ref, cw_ref, cb_ref, w0_ref, wup_ref,
                    a0_ref, aup_ref, kk_ref, ka_ref, rk_ref, seg_ref, tri_ref,
                    slab_ref, decay_ref, v_ref, bonus_ref):
    tm = main_ref.shape[0]
    i = pl.program_id(0)
    local = i % tiles_per_seq
    is_first = local == 0
    is_last = local == tiles_per_seq - 1

    xm = main_ref[...]
    row = lax.broadcasted_iota(jnp.int32, (tm, 1), 0)
    prev_row = jnp.where(is_first, 0.0, prev_ref[7:8, :])
    next_row = jnp.where(is_last, 0.0, next_ref[0:1, :])
    xp = jnp.where(row == 0, prev_row, pltpu.roll(xm, 1, 0))
    xn = jnp.where(row == tm - 1, next_row, pltpu.roll(xm, tm - 1, 0))
    cw = cw_ref[...]
    c = cb_ref[...] + xp * cw[0:1] + xm * cw[1:2] + xn * cw[2:3]

    r = c[:, 0:512]
    k = c[:, 512:1024]
    v = c[:, 1024:1536]
    zw = c[:, 1536:1664]
    za = c[:, 1664:1792]

    wl = w0_ref[...] + _dot(jnp.tanh(zw).astype(bf16), wup_ref[...])
    logw = -DECAY_SCALE * _sigmoid(wl)
    a = _sigmoid(a0_ref[...] + _dot(za.astype(bf16), aup_ref[...]))
    seg = seg_ref[...]
    kk = k * kk_ref[...]
    kk = kk * lax.rsqrt(_segsum(kk * kk, seg) + KK_EPS)
    kmod = k * (1.0 + (a - 1.0) * ka_ref[...])
    kka = kk * a
    bonus_ref[...] = _segsum(r * kmod * rk_ref[...], seg) * v
    v_ref[...] = v.astype(bf16)

    for d in range(2):
        lw = logw[:, d * 512:(d + 1) * 512]
        cums = _dot_exact01(tri_ref[d], lw, 3)
        cum_in = cums[:tm]
        rev_ex = cums[tm:]
        e_pos = jnp.exp(cum_in)
        e_neg = jnp.exp(-cum_in)
        e_rev = jnp.exp(rev_ex)
        slab_ref[d, 0] = (r * e_pos).astype(bf16)
        slab_ref[d, 1] = (kmod * e_neg).astype(bf16)
        slab_ref[d, 2] = (-kk * jnp.exp(cum_in - lw)).astype(bf16)
        slab_ref[d, 3] = (kka * e_neg).astype(bf16)
        slab_ref[d, 4] = (kka * e_rev).astype(bf16)
        slab_ref[d, 5] = (kmod * e_rev).astype(bf16)
        total = jnp.exp(cum_in + rev_ex)
        for c in range(tm // CHUNK):
            decay_ref[d, c] = total[c * CHUNK:c * CHUNK + 1, :]


def _rwkv_prep(conv_in, seq_len, prm):
    T = conv_in.shape[0]
    tm = TOKEN_TILE
    tiles_per_seq = seq_len // tm
    n8 = T // 8
    const = lambda shape: pl.BlockSpec(shape, lambda i: (0,) * len(shape))
    return pl.pallas_call(
        functools.partial(_rwkv_prep_body, tiles_per_seq),
        grid=(T // tm,),
        in_specs=[
            pl.BlockSpec((tm, CONV_PAD), lambda i: (i, 0)),
            pl.BlockSpec((8, CONV_PAD), lambda i: (jnp.maximum(i * (tm // 8) - 1, 0), 0)),
            pl.BlockSpec((8, CONV_PAD), lambda i: (jnp.minimum((i + 1) * (tm // 8), n8 - 1), 0)),
            const((3, CONV_PAD)), const((1, CONV_PAD)), const((1, 1024)), const((128, 1024)),
            const((1, 512)), const((128, 512)), const((1, 512)), const((1, 512)), const((1, 512)),
            const((512, 512)), const((2, 2 * tm, tm)),
        ],
        out_specs=[
            pl.BlockSpec((2, 6, tm, 512), lambda i: (0, 0, i, 0)),
            pl.BlockSpec((2, tm // CHUNK, 1, 512), lambda i: (0, i, 0, 0)),
            pl.BlockSpec((tm, 512), lambda i: (i, 0)),
            pl.BlockSpec((tm, 512), lambda i: (i, 0)),
        ],
        out_shape=[
            jax.ShapeDtypeStruct((2, 6, T, 512), bf16),
            jax.ShapeDtypeStruct((2, T // CHUNK, 1, 512), f32),
            jax.ShapeDtypeStruct((T, 512), bf16),
            jax.ShapeDtypeStruct((T, 512), f32),
        ],
        compiler_params=pltpu.CompilerParams(
            dimension_semantics=("arbitrary",), vmem_limit_bytes=VMEM_LIMIT),
        name="rwkv_prep",
    )(conv_in, conv_in, conv_in, prm["conv_w"], prm["conv_b"], prm["w0"], prm["w_up"],
      prm["a0"], prm["a_up"], prm["k_k"], prm["k_a"], prm["r_k"], prm["seg"], prm["tri"])


def _mm(a, b):
    return _dot(a.astype(bf16), b.astype(bf16))


def _stack_heads(x, head_masks):
    x = x.astype(bf16)
    return jnp.concatenate([x * hm for hm in head_masks], axis=0)


def _rwkv_scan_body(slab_f_ref, slab_b_ref, decay_f_ref, decay_b_ref, v_f_ref, v_b_ref, y_f_ref, y_b_ref,
                    z_ref):
    L = CHUNK
    n_groups = RW_WIDTH // GROUP

    @pl.when(pl.program_id(1) == 0)
    def _():
        z_ref[...] = jnp.zeros_like(z_ref)

    t_row = lax.broadcasted_iota(jnp.int32, (L, GROUP), 0)
    t_col = lax.broadcasted_iota(jnp.int32, (L, GROUP), 1) % L
    eye_cat = jnp.where(t_row == t_col, 1.0, 0.0)
    strict = [(t_col < t_row).astype(f32), (t_col > t_row).astype(f32)]
    incl = [(t_col <= t_row).astype(f32), (t_col >= t_row).astype(f32)]
    rg = lax.broadcasted_iota(jnp.int32, (GROUP, GROUP), 0)
    cg = lax.broadcasted_iota(jnp.int32, (GROUP, GROUP), 1)
    blockdiag = ((rg // HEAD_DIM) == (cg // HEAD_DIM)).astype(f32)
    eye_g = rg == cg
    lane = lax.broadcasted_iota(jnp.int32, (1, GROUP), 1) // HEAD_DIM
    head_masks = [(lane == h).astype(bf16) for h in range(HEADS_PER_GROUP)]
    stack = lambda x: _stack_heads(x, head_masks)

    chains = []
    for d, (slab_ref, decay_ref, v_ref) in enumerate(
            ((slab_f_ref, decay_f_ref, v_f_ref), (slab_b_ref, decay_b_ref, v_b_ref))):
        for sub in (range(CHUNKS_PER_STEP) if d == 0 else reversed(range(CHUNKS_PER_STEP))):
            rows = slice(sub * L, (sub + 1) * L)
            for g in range(n_groups):
                lanes = slice(g * GROUP, (g + 1) * GROUP)
                chains.append(dict(
                    d=d, g=g, lanes=lanes, rows=rows,
                    r=slab_ref[0, rows, lanes], k=slab_ref[1, rows, lanes], a=slab_ref[2, rows, lanes],
                    b=slab_ref[3, rows, lanes], bp=slab_ref[4, rows, lanes], kp=slab_ref[5, rows, lanes],
                    pdiag=decay_ref[sub, :, lanes], v=v_ref[rows, lanes]))

    for ch in chains:
        d = ch["d"]
        ch["v_s"] = stack(ch["v"])
        gram = _dot_nt(jnp.concatenate([ch["a"], ch["r"]], axis=0),
                       jnp.concatenate([stack(ch["b"]), stack(ch["k"])], axis=0))
        ch["m_ab"] = gram[:L, :GROUP] * strict[d]
        ch["m_ak"] = gram[:L, GROUP:] * strict[d]
        ch["m_rb"] = gram[L:, :GROUP] * incl[d]
        ch["m_rk"] = gram[L:, GROUP:] * incl[d]
    for ch in chains:
        ch["t"] = eye_cat + ch["m_ab"]
        ch["q"] = _mm(ch["m_ab"], stack(ch["m_ab"]))
    for _ in range(4):
        for ch in chains:
            p = _mm(jnp.concatenate([ch["t"], ch["q"]], axis=0), stack(ch["q"]))
            ch["t"] = ch["t"] + p[:L]
            ch["q"] = p[L:]
    for ch in chains:
        ch["t"] = ch["t"] + _mm(ch["t"], stack(ch["q"]))
    for ch in chains:
        ch["w1"] = _mm(ch["m_ak"], ch["v_s"])
    for ch in chains:
        au = _mm(ch["t"], jnp.concatenate([stack(ch["a"]), stack(ch["w1"])], axis=1))
        ch["a_hat"] = au[:, :GROUP]
        ch["u_hat"] = au[:, GROUP:]
    for ch in chains:
        o1 = _mm(ch["m_rb"], jnp.concatenate([stack(ch["a_hat"]), stack(ch["u_hat"])], axis=1))
        ch["r_hat"] = ch["r"].astype(f32) + o1[:, :GROUP]
        ch["y_hat"] = o1[:, GROUP:] + _mm(ch["m_rk"], ch["v_s"])
    for ch in chains:
        bp_t = ch["bp"].astype(f32).T.astype(bf16)
        kp_t = ch["kp"].astype(f32).T.astype(bf16)
        ch["phi_t"] = _dot(bp_t, ch["a_hat"].astype(bf16)) * blockdiag + jnp.where(eye_g, ch["pdiag"], 0.0)
        ch["psi_t"] = _dot(jnp.concatenate([bp_t, kp_t], axis=1),
                           jnp.concatenate([ch["u_hat"].astype(bf16), ch["v"]], axis=0)) * blockdiag
    y_refs = (y_f_ref, y_b_ref)
    for ch in chains:
        z = z_ref[ch["d"], ch["g"]].astype(bf16)
        y_refs[ch["d"]][ch["rows"], ch["lanes"]] = _dot(ch["r_hat"].astype(bf16), z) + ch["y_hat"]
        z_ref[ch["d"], ch["g"]] = _dot(ch["phi_t"].astype(bf16), z) + ch["psi_t"]


def _rwkv_scan(slab, decay, v, batch, seq_len):
    T = v.shape[0]
    rows = CHUNKS_PER_STEP * CHUNK
    steps = seq_len // rows
    fwd = lambda b, c: b * steps + c
    bwd = lambda b, c: b * steps + (steps - 1 - c)
    return pl.pallas_call(
        _rwkv_scan_body,
        grid=(batch, steps),
        in_specs=[
            pl.BlockSpec((None, 6, rows, 512), lambda b, c: (0, 0, fwd(b, c), 0)),
            pl.BlockSpec((None, 6, rows, 512), lambda b, c: (1, 0, bwd(b, c), 0)),
            pl.BlockSpec((None, CHUNKS_PER_STEP, 1, 512), lambda b, c: (0, fwd(b, c), 0, 0)),
            pl.BlockSpec((None, CHUNKS_PER_STEP, 1, 512), lambda b, c: (1, bwd(b, c), 0, 0)),
            pl.BlockSpec((rows, 512), lambda b, c: (fwd(b, c), 0)),
            pl.BlockSpec((rows, 512), lambda b, c: (bwd(b, c), 0)),
        ],
        out_specs=[
            pl.BlockSpec((rows, 512), lambda b, c: (fwd(b, c), 0)),
            pl.BlockSpec((rows, 512), lambda b, c: (bwd(b, c), 0)),
        ],
        out_shape=[jax.ShapeDtypeStruct((T, 512), f32), jax.ShapeDtypeStruct((T, 512), f32)],
        scratch_shapes=[pltpu.VMEM((2, RW_WIDTH // GROUP, GROUP, GROUP), f32)],
        compiler_params=pltpu.CompilerParams(
            dimension_semantics=("arbitrary", "arbitrary"), vmem_limit_bytes=VMEM_LIMIT),
        name="rwkv_scan",
    )(slab, slab, decay, decay, v, v)


def _attn_prep_body(rest_ref, qn_ref, kn_ref, seg_ref, q0_ref, q1_ref, k_ref, v_ref):
    seg = seg_ref[...]
    q = rest_ref[:, 512:1024]
    k = rest_ref[:, 1024:1536]
    v = rest_ref[:, 1536:2048]
    inv_d = 1.0 / HEAD_DIM
    qms = _segsum(q * q, seg) * inv_d
    kms = _segsum(k * k, seg) * inv_d
    qn = q * lax.rsqrt(qms + NORM_EPS) * qn_ref[...] * (HEAD_DIM ** -0.5)
    kn = k * lax.rsqrt(kms + NORM_EPS) * kn_ref[...]
    lane = lax.broadcasted_iota(jnp.int32, (1, DA_WIDTH), 1)
    first_map = (lane // HEAD_DIM) % 2 == 0
    q0_ref[...] = jnp.where(first_map, qn, 0.0).astype(bf16)
    q1_ref[...] = jnp.where(first_map, 0.0, qn).astype(bf16)
    k_ref[...] = kn.astype(bf16)
    v_ref[...] = v.astype(bf16)


def _attn_prep(rest, prm):
    T = rest.shape[0]
    tm = TOKEN_TILE
    out = jax.ShapeDtypeStruct((T, DA_WIDTH), bf16)
    spec = pl.BlockSpec((tm, DA_WIDTH), lambda i: (i, 0))
    return pl.pallas_call(
        _attn_prep_body,
        grid=(T // tm,),
        in_specs=[
            pl.BlockSpec((tm, REST_COLS), lambda i: (i, 0)),
            pl.BlockSpec((1, DA_WIDTH), lambda i: (0, 0)),
            pl.BlockSpec((1, DA_WIDTH), lambda i: (0, 0)),
            pl.BlockSpec((512, 512), lambda i: (0, 0)),
        ],
        out_specs=[spec, spec, spec, spec],
        out_shape=[out, out, out, out],
        compiler_params=pltpu.CompilerParams(
            dimension_semantics=("arbitrary",), vmem_limit_bytes=VMEM_LIMIT),
        name="attn_prep",
    )(rest, prm["q_norm"], prm["k_norm"], prm["seg"])


def _flash_body(lam_init, tq, tk, seq_len, reach_ref, q0_ref, q1_ref, qaug_ref, k_ref, kaug_ref, v_ref,
                lam_ref, subln_ref, o_ref, s_ref, acc_ref, bias_ref):
    h = pl.program_id(1)
    qi = pl.program_id(2)
    nk = seq_len // tk
    lane_blocks = tk // DA_V_DIM

    kd = (qi * tq) // tk
    q_main = (q0_ref[...], q1_ref[...])
    q_aug = qaug_ref[...].astype(f32)
    ones_col = jnp.where(lax.broadcasted_iota(jnp.int32, (2 * tk, DA_V_DIM), 1) == 0, 1.0, 0.0).astype(bf16)

    slope = jnp.exp2(jnp.full((1, 1), -(ALIBI_MAX_BIAS / DA_HEADS), f32) * (h + 1).astype(f32))
    rel = (lax.broadcasted_iota(jnp.int32, (tq, tk), 0) - lax.broadcasted_iota(jnp.int32, (tq, tk), 1)
           + (qi * tq - kd * tk))
    bias_ref[...] = slope * jnp.abs(rel).astype(f32)

    def tile_start(kt):
        return kt * tk if isinstance(kt, int) else pl.multiple_of(kt * tk, tk)

    def scores(kt, running):
        start = tile_start(kt)
        k = jnp.concatenate([k_ref[pl.ds(start, tk), :], kaug_ref[pl.ds(start, tk), :]], axis=1)
        sign = jnp.where(kt < kd, 1.0, jnp.where(kt > kd, -1.0, 0.0)).astype(f32)
        aug = (q_aug * sign).astype(bf16)
        explicit_bias = jnp.where(kt == kd, 1.0, 0.0).astype(f32) * bias_ref[...]
        out = []
        for mi in range(2):
            s = _dot_nt(jnp.concatenate([q_main[mi], aug], axis=1), k) - explicit_bias
            s_ref[mi, kt] = s
            mx = running[mi]
            for c in range(lane_blocks):
                mx = jnp.maximum(mx, s[:, c * DA_V_DIM:(c + 1) * DA_V_DIM])
            out.append(mx)
        return tuple(out)

    span = 2 * tk
    reach = reach_ref[h]
    first_pair = jnp.maximum(qi * tq - reach, 0) // span
    n_pairs = jnp.minimum((qi * tq + tq - 1 + reach) // span + 1, nk // 2) - first_pair

    def score_pair(j, running):
        return scores(2 * j + 1, scores(2 * j, running))

    running = (jnp.full((tq, DA_V_DIM), -jnp.inf, f32),) * 2
    running = lax.fori_loop(
        0, n_pairs // 2,
        lambda i, r: score_pair(first_pair + 2 * i + 1, score_pair(first_pair + 2 * i, r)), running)
    running = lax.fori_loop(
        0, n_pairs % 2, lambda i, r: score_pair(first_pair + n_pairs - 1, r), running)
    if nk % 2:
        running = scores(nk - 1, running)
    row_max = [jnp.broadcast_to(jnp.max(mx, axis=-1, keepdims=True), (tq, DA_V_DIM)) for mx in running]

    acc_ref[...] = jnp.zeros_like(acc_ref)

    def weighted_values(mi, first, count):
        v = jnp.concatenate([v_ref[pl.ds(tile_start(first), count * tk), :], ones_col[:count * tk]], axis=1)
        ps = [jnp.exp(s_ref[mi, first + j, :, c * DA_V_DIM:(c + 1) * DA_V_DIM] - row_max[mi]).astype(bf16)
              for j in range(count) for c in range(lane_blocks)]
        return _dot(jnp.concatenate(ps, axis=1), v)

    def two_pairs(i, carry):
        j = first_pair + 2 * i
        for mi in range(2):
            acc_ref[mi] += weighted_values(mi, 2 * j, 2) + weighted_values(mi, 2 * j + 2, 2)
        return carry

    def last_pair(i, carry):
        for mi in range(2):
            acc_ref[mi] += weighted_values(mi, 2 * (first_pair + n_pairs - 1), 2)
        return carry

    lax.fori_loop(0, n_pairs // 2, two_pairs, 0)
    lax.fori_loop(0, n_pairs % 2, last_pair, 0)
    if nk % 2:
        for mi in range(2):
            acc_ref[mi] += weighted_values(mi, nk - 1, 1)

    lv = lam_ref[...]
    lam = (jnp.exp(jnp.sum(lv[0:1] * lv[1:2], axis=-1, keepdims=True))
           - jnp.exp(jnp.sum(lv[2:3] * lv[3:4], axis=-1, keepdims=True)) + lam_init)
    acc0 = acc_ref[0]
    acc1 = acc_ref[1]
    o = (acc0[:, :DA_V_DIM] / acc0[:, DA_V_DIM:DA_V_DIM + 1]
         - lam * (acc1[:, :DA_V_DIM] / acc1[:, DA_V_DIM:DA_V_DIM + 1]))
    ms = jnp.mean(o * o, axis=-1, keepdims=True)
    o_ref[...] = o * lax.rsqrt(ms + NORM_EPS) * subln_ref[...] * (1.0 - lam_init)


def _alibi_reach(q_norm, k_norm):
    gq = jnp.max(jnp.abs(q_norm))
    gk = jnp.max(jnp.abs(k_norm))
    slopes = 2.0 ** (-(ALIBI_MAX_BIAS / DA_HEADS) * jnp.arange(1, DA_HEADS + 1, dtype=f32))
    dist = (1.05 * 16.0 * gq * gk + EXP2_UNDERFLOW * math.log(2.0)) / slopes
    return jnp.minimum(jnp.ceil(dist), float(MAX_REACH)).astype(jnp.int32)


def _alibi_columns(seq_len):
    pos = jnp.arange(seq_len, dtype=jnp.int32)
    hi = (pos // POS_SPLIT).astype(f32)[None, :, None]
    lo = (pos % POS_SPLIT).astype(f32)[None, :, None]
    slopes = (2.0 ** (-(ALIBI_MAX_BIAS / DA_HEADS) * jnp.arange(1, DA_HEADS + 1, dtype=f32)))[:, None, None]
    col = jnp.arange(DA_V_DIM, dtype=jnp.int32)[None, None, :]
    q_cols = jnp.where(col == 0, -slopes * POS_SPLIT * hi,
                       jnp.where(col == 1, -slopes * lo, jnp.where(col < 4, 1.0, 0.0)))
    k_cols = jnp.where(col < 2, 1.0,
                       jnp.where(col == 2, slopes * POS_SPLIT * hi, jnp.where(col == 3, slopes * lo, 0.0)))
    return q_cols.astype(bf16), k_cols.astype(bf16)


def _flash(q0, q1, k, v, reach, lam_vec, subln, lam_init, batch, seq_len):
    T = q0.shape[0]
    tq = min(FLASH_TQ, seq_len)
    tk = min(FLASH_TK, seq_len)
    nq = seq_len // tq
    q_cols, k_cols = _alibi_columns(seq_len)
    qspec = pl.BlockSpec((tq, DA_V_DIM), lambda b, h, i, reach_ref: (b * nq + i, h))
    kspec = pl.BlockSpec((seq_len, DA_V_DIM), lambda b, h, i, reach_ref: (b, h))
    return pl.pallas_call(
        functools.partial(_flash_body, lam_init, tq, tk, seq_len),
        grid_spec=pltpu.PrefetchScalarGridSpec(
            num_scalar_prefetch=1,
            grid=(batch, DA_HEADS, nq),
            in_specs=[qspec, qspec,
                      pl.BlockSpec((None, tq, DA_V_DIM), lambda b, h, i, reach_ref: (h, i, 0)),
                      kspec,
                      pl.BlockSpec((None, seq_len, DA_V_DIM), lambda b, h, i, reach_ref: (h, 0, 0)),
                      kspec,
                      pl.BlockSpec((4, HEAD_DIM), lambda b, h, i, reach_ref: (0, 0)),
                      pl.BlockSpec((1, DA_V_DIM), lambda b, h, i, reach_ref: (0, 0))],
            out_specs=pl.BlockSpec((tq, DA_V_DIM), lambda b, h, i, reach_ref: (b * nq + i, h)),
            scratch_shapes=[pltpu.VMEM((2, seq_len // tk, tq, tk), f32), pltpu.VMEM((2, tq, DA_EXT), f32),
                            pltpu.VMEM((tq, tk), f32)],
        ),
        out_shape=jax.ShapeDtypeStruct((T, DA_WIDTH), f32),
        compiler_params=pltpu.CompilerParams(
            dimension_semantics=("arbitrary", "arbitrary", "arbitrary"),
            vmem_limit_bytes=VMEM_LIMIT),
        name="flash_diff_attn",
    )(reach, q0, q1, q_cols, k, k_cols, v, lam_vec, subln)


def _out_body(h_ref, yf_ref, yb_ref, bonus_ref, grw_ref, gda_ref, o_ref, p_ref, seg_ref, lng_ref, lnb_ref,
              wout_ref, pproj_ref, pnorm_ref, gw_ref, gb_ref, out_ref):
    seg = seg_ref[...]
    inv_n = 1.0 / HEAD_DIM
    y = yf_ref[...] + yb_ref[...]
    mu = _segsum(y, seg) * inv_n
    yc = y - mu
    var = _segsum(yc * yc, seg) * inv_n
    y_rw = (yc * lax.rsqrt(var + LN_X_EPS) * lng_ref[...] + lnb_ref[...] + bonus_ref[...])
    y_rw = y_rw * _silu(grw_ref[...])
    y_da = o_ref[...] * _silu(gda_ref[...])
    h1 = (h_ref[...] + _dot(y_rw.astype(bf16), wout_ref[0:512, :])
          + _dot(y_da.astype(bf16), wout_ref[512:1024, :]))
    e = _dot(p_ref[...].astype(bf16), pproj_ref[...])
    e = e * lax.rsqrt(jnp.mean(e * e, axis=-1, keepdims=True) + NORM_EPS) * pnorm_ref[...]
    gate = _sigmoid(_dot(h1.astype(bf16), gw_ref[...]) + gb_ref[...])
    out_ref[...] = h1 + gate * e


def _out_stage(h, y_f, y_b, bonus, rest, o, p, prm):
    T = h.shape[0]
    tm = TOKEN_TILE
    row = lambda w: pl.BlockSpec((tm, w), lambda i: (i, 0))
    const = lambda shape: pl.BlockSpec(shape, lambda i: (0,) * len(shape))
    return pl.pallas_call(
        _out_body,
        grid=(T // tm,),
        in_specs=[row(D_MODEL), row(512), row(512), row(512),
                  pl.BlockSpec((tm, 512), lambda i: (i, 0)),
                  pl.BlockSpec((tm, 512), lambda i: (i, 4)),
                  row(512), row(D_PLE),
                  const((512, 512)), const((1, 512)), const((1, 512)),
                  const((D_MODEL, D_MODEL)), const((D_PLE, D_MODEL)), const((1, D_MODEL)),
                  const((D_MODEL, D_MODEL)), const((1, D_MODEL))],
        out_specs=row(D_MODEL),
        out_shape=jax.ShapeDtypeStruct((T, D_MODEL), f32),
        compiler_params=pltpu.CompilerParams(
            dimension_semantics=("arbitrary",), vmem_limit_bytes=VMEM_LIMIT),
        name="out_stage",
    )(h, y_f, y_b, bonus, rest, rest, o, p, prm["seg"], prm["ln_g"], prm["ln_b"], prm["w_out"],
      prm["ple_proj"], prm["ple_norm"], prm["gate_w"], prm["gate_b"])


def _chunk_triangles(tm):
    t = np.arange(tm)
    same = (t[:, None] // CHUNK) == (t[None, :] // CHUNK)
    low_incl = same & (t[None, :] <= t[:, None])
    up_strict = same & (t[None, :] > t[:, None])
    up_incl = same & (t[None, :] >= t[:, None])
    low_strict = same & (t[None, :] < t[:, None])
    tri = np.stack([np.concatenate([low_incl, up_strict], 0), np.concatenate([up_incl, low_strict], 0)])
    return jnp.asarray(tri.astype(np.float32), dtype=bf16)


def _layer_params(i, norm_pre, w_in, w_out, rw_conv_w, rw_conv_b, rw_w0, rw_w_up, rw_a0, rw_a_up, rw_k_k,
                  rw_k_a, rw_r_k, rw_ln_g, rw_ln_b, da_q_norm, da_k_norm, da_lambda, da_subln, ple_proj,
                  ple_norm, ple_gate_w, ple_gate_b):
    pad = CONV_PAD - (CONV_COLS)
    w = w_in[i]
    w_pad = jnp.concatenate(
        [w[:, :CONV_COLS], jnp.zeros((D_MODEL, pad), f32), w[:, CONV_COLS:]], axis=1).astype(bf16)
    wup = rw_w_up[i]
    zeros = jnp.zeros((W_LORA, RW_WIDTH), f32)
    w_up_bd = jnp.concatenate(
        [jnp.concatenate([wup[0], zeros], 1), jnp.concatenate([zeros, wup[1]], 1)], 0).astype(bf16)
    a_up_pad = jnp.concatenate([rw_a_up[i], jnp.zeros((64, RW_WIDTH), f32)], 0).astype(bf16)
    lane = np.arange(512)
    seg = jnp.asarray((lane[:, None] // HEAD_DIM == lane[None, :] // HEAD_DIM).astype(np.float32), dtype=bf16)
    return dict(
        norm_pre=norm_pre[i].reshape(1, D_MODEL),
        w_in=w_pad,
        conv_w=jnp.pad(rw_conv_w[i], ((0, 0), (0, pad))),
        conv_b=jnp.pad(rw_conv_b[i], (0, pad)).reshape(1, CONV_PAD),
        w0=rw_w0[i].reshape(1, 2 * RW_WIDTH),
        w_up=w_up_bd,
        a0=rw_a0[i].reshape(1, RW_WIDTH),
        a_up=a_up_pad,
        k_k=rw_k_k[i].reshape(1, RW_WIDTH),
        k_a=rw_k_a[i].reshape(1, RW_WIDTH),
        r_k=rw_r_k[i].reshape(1, RW_WIDTH),
        ln_g=rw_ln_g[i].reshape(1, RW_WIDTH),
        ln_b=rw_ln_b[i].reshape(1, RW_WIDTH),
        q_norm=jnp.tile(da_q_norm[i].reshape(1, 2 * HEAD_DIM), (1, DA_HEADS)),
        k_norm=jnp.tile(da_k_norm[i].reshape(1, 2 * HEAD_DIM), (1, DA_HEADS)),
        reach=_alibi_reach(da_q_norm[i], da_k_norm[i]),
        lam_vec=da_lambda[i],
        subln=da_subln[i].reshape(1, DA_V_DIM),
        w_out=w_out[i].astype(bf16),
        ple_proj=ple_proj[i].astype(bf16),
        ple_norm=ple_norm[i].reshape(1, D_MODEL),
        gate_w=ple_gate_w[i].astype(bf16),
        gate_b=ple_gate_b[i].reshape(1, D_MODEL),
        seg=seg,
        tri=_chunk_triangles(TOKEN_TILE),
    )


def _layer(h, p, lam_init, prm, batch, seq_len):
    conv_in, rest = _inproj(h, prm["norm_pre"], prm["w_in"])
    slab, decay, v_rw, bonus = _rwkv_prep(conv_in, seq_len, prm)
    y_f, y_b = _rwkv_scan(slab, decay, v_rw, batch, seq_len)
    q0, q1, k, v = _attn_prep(rest, prm)
    o = _flash(q0, q1, k, v, prm["reach"], prm["lam_vec"], prm["subln"], lam_init, batch, seq_len)
    return _out_stage(h, y_f, y_b, bonus, rest, o, p, prm)


def _trunk(x, p, layers):
    batch, seq_len, _ = x.shape
    h = x.reshape(batch * seq_len, D_MODEL)
    for i, prm in enumerate(layers):
        lam_init = 0.8 - 0.6 * math.exp(-0.3 * i)
        h = _layer(h, p[i].reshape(batch * seq_len, D_PLE), lam_init, prm, batch, seq_len)
    return h.reshape(batch, seq_len, D_MODEL)


def kernel(x_prompt, x_sample, p_prompt, p_sample, norm_pre, w_in, w_out, rw_conv_w, rw_conv_b, rw_w0, rw_w_up, rw_a0, rw_a_up, rw_k_k, rw_k_a, rw_r_k, rw_ln_g, rw_ln_b, da_q_norm, da_k_norm, da_lambda, da_subln, ple_proj, ple_norm, ple_gate_w, ple_gate_b):
    depth = norm_pre.shape[0]
    layers = [_layer_params(i, norm_pre, w_in, w_out, rw_conv_w, rw_conv_b, rw_w0, rw_w_up, rw_a0, rw_a_up,
                            rw_k_k, rw_k_a, rw_r_k, rw_ln_g, rw_ln_b, da_q_norm, da_k_norm, da_lambda,
                            da_subln, ple_proj, ple_norm, ple_gate_w, ple_gate_b) for i in range(depth)]
    return (_trunk(x_prompt, p_prompt, layers), _trunk(x_sample, p_sample, layers))
```

```python
import functools
import math

import numpy as np
import jax
import jax.numpy as jnp
from jax import lax
from jax.experimental import pallas as pl
from jax.experimental.pallas import tpu as pltpu

f32 = jnp.float32
bf16 = jnp.bfloat16

D_MODEL = 1024
D_PLE = 256
RW_WIDTH = 512
HEAD_DIM = 64
W_LORA = 64
CONV_COLS = 3 * RW_WIDTH + 2 * W_LORA + 64
CONV_PAD = 1792
DA_HEADS = 4
DA_V_DIM = 128
DA_WIDTH = 512
DA_EXT = 256
GATE_COLS = 2 * 512
NORM_EPS = 1e-6
LN_X_EPS = 64e-5
KK_EPS = 1e-12
ALIBI_MAX_BIAS = 8.0
DECAY_SCALE = math.exp(-0.5)
LOG2_E = math.log2(math.e)
EXP2_UNDERFLOW = 160.0
MAX_REACH = 1 << 24
POS_SPLIT = 128

CHUNK = 64
CHUNKS_PER_STEP = 4
GROUP = 256
HEADS_PER_GROUP = GROUP // HEAD_DIM
STACK = HEADS_PER_GROUP * CHUNK
TOKEN_TILE = 256
FLASH_TQ = 256
FLASH_TK = 512
VMEM_LIMIT = 48 * 1024 * 1024


def _dot(a, b):
    return jnp.dot(a, b, preferred_element_type=f32)


def _dot_nt(a, b):
    return lax.dot_general(a, b, (((1,), (1,)), ((), ())), preferred_element_type=f32)


def _split2(x):
    hi = x.astype(bf16)
    lo = (x - hi.astype(f32)).astype(bf16)
    return hi, lo


def _split3(x):
    hi = x.astype(bf16)
    r1 = x - hi.astype(f32)
    mid = r1.astype(bf16)
    lo = (r1 - mid.astype(f32)).astype(bf16)
    return hi, mid, lo


def _dot_exact01(m01, x, parts):
    ps = _split3(x) if parts == 3 else _split2(x)
    acc = _dot(m01, ps[0])
    for p in ps[1:]:
        acc = acc + _dot(m01, p)
    return acc


def _segsum(x, seg):
    hi, lo = _split2(x)
    return _dot(hi, seg) + _dot(lo, seg)


def _sigmoid(x):
    return 1.0 / (1.0 + jnp.exp(-x))


def _silu(x):
    return x * _sigmoid(x)


def _inproj_body(x_ref, g_ref, w_ref, qn_ref, kn_ref, seg_ref, conv_ref, gates_ref, q0_ref, q1_ref, k_ref,
                 v_ref):
    x = x_ref[...]
    ms = jnp.mean(x * x, axis=-1, keepdims=True)
    u = (x * lax.rsqrt(ms + NORM_EPS) * g_ref[...]).astype(bf16)
    for c0 in range(0, CONV_PAD, 256):
        conv_ref[:, c0:c0 + 256] = _dot(u, w_ref[:, c0:c0 + 256])
    for c0 in range(0, GATE_COLS, 256):
        gates_ref[:, c0:c0 + 256] = _dot(u, w_ref[:, CONV_PAD + c0:CONV_PAD + c0 + 256])
    base = CONV_PAD + GATE_COLS
    q = _dot(u, w_ref[:, base:base + DA_WIDTH])
    k = _dot(u, w_ref[:, base + DA_WIDTH:base + 2 * DA_WIDTH])
    v_ref[...] = _dot(u, w_ref[:, base + 2 * DA_WIDTH:base + 3 * DA_WIDTH]).astype(bf16)
    seg = seg_ref[...]
    inv_d = 1.0 / HEAD_DIM
    qn = q * lax.rsqrt(_segsum(q * q, seg) * inv_d + NORM_EPS) * qn_ref[...] * (HEAD_DIM ** -0.5)
    kn = k * lax.rsqrt(_segsum(k * k, seg) * inv_d + NORM_EPS) * kn_ref[...]
    lane = lax.broadcasted_iota(jnp.int32, (1, DA_WIDTH), 1)
    first_map = (lane // HEAD_DIM) % 2 == 0
    q0_ref[...] = jnp.where(first_map, qn, 0.0).astype(bf16)
    q1_ref[...] = jnp.where(first_map, 0.0, qn).astype(bf16)
    k_ref[...] = kn.astype(bf16)


def _inproj(h, prm):
    T = h.shape[0]
    tm = TOKEN_TILE
    row = lambda w: pl.BlockSpec((tm, w), lambda i: (i, 0))
    const = lambda shape: pl.BlockSpec(shape, lambda i: (0,) * len(shape))
    attn = jax.ShapeDtypeStruct((T, DA_WIDTH), bf16)
    return pl.pallas_call(
        _inproj_body,
        grid=(T // tm,),
        in_specs=[row(D_MODEL), const((1, D_MODEL)), const((D_MODEL, CONV_PAD + GATE_COLS + 3 * DA_WIDTH)),
                  const((1, DA_WIDTH)), const((1, DA_WIDTH)), const((512, 512))],
        out_specs=[row(CONV_PAD), row(GATE_COLS), row(DA_WIDTH), row(DA_WIDTH), row(DA_WIDTH), row(DA_WIDTH)],
        out_shape=[jax.ShapeDtypeStruct((T, CONV_PAD), f32), jax.ShapeDtypeStruct((T, GATE_COLS), f32),
                   attn, attn, attn, attn],
        compiler_params=pltpu.CompilerParams(
            dimension_semantics=("arbitrary",), vmem_limit_bytes=VMEM_LIMIT),
        name="inproj",
    )(h, prm["norm_pre"], prm["w_in"], prm["q_norm"], prm["k_norm"], prm["seg"])


def _rwkv_prep_body(tiles_per_seq, main_ref, prev_ref, next_ref, cw_ref, cb_ref, w0_ref, wup_ref,
                    a0_ref, aup_ref, kk_ref, ka_ref, rk_ref, seg_ref, tri_ref,
                    slab_ref, decay_ref, v_ref, bonus_ref):
    tm = main_ref.shape[0]
    i = pl.program_id(0)
    local = i % tiles_per_seq
    is_first = local == 0
    is_last = local == tiles_per_seq - 1

    xm = main_ref[...]
    row = lax.broadcasted_iota(jnp.int32, (tm, 1), 0)
    prev_row = jnp.where(is_first, 0.0, prev_ref[7:8, :])
    next_row = jnp.where(is_last, 0.0, next_ref[0:1, :])
    xp = jnp.where(row == 0, prev_row, pltpu.roll(xm, 1, 0))
    xn = jnp.where(row == tm - 1, next_row, pltpu.roll(xm, tm - 1, 0))
    cw = cw_ref[...]
    c = cb_ref[...] + xp * cw[0:1] + xm * cw[1:2] + xn * cw[2:3]

    r = c[:, 0:512]
    k = c[:, 512:1024]
    v = c[:, 1024:1536]
    zw = c[:, 1536:1664]
    za = c[:, 1664:1792]

    wl = w0_ref[...] + _dot(jnp.tanh(zw).astype(bf16), wup_ref[...])
    logw = -DECAY_SCALE * _sigmoid(wl)
    a = _sigmoid(a0_ref[...] + _dot(za.astype(bf16), aup_ref[...]))
    seg = seg_ref[...]
    kk = k * kk_ref[...]
    kk = kk * lax.rsqrt(_segsum(kk * kk, seg) + KK_EPS)
    kmod = k * (1.0 + (a - 1.0) * ka_ref[...])
    kka = kk * a
    bonus_ref[...] = _segsum(r * kmod * rk_ref[...], seg) * v
    v_ref[...] = v.astype(bf16)

    for d in range(2):
        lw = logw[:, d * 512:(d + 1) * 512]
        cums = _dot_exact01(tri_ref[d], lw, 2)
        cum_in = cums[:tm]
        rev_ex = cums[tm:]
        e_pos = jnp.exp(cum_in)
        e_neg = jnp.exp(-cum_in)
        e_rev = jnp.exp(rev_ex)
        slab_ref[d, 0] = (r * e_pos).astype(bf16)
        slab_ref[d, 1] = (kmod * e_neg).astype(bf16)
        slab_ref[d, 2] = (-kk * jnp.exp(cum_in - lw)).astype(bf16)
        slab_ref[d, 3] = (kka * e_neg).astype(bf16)
        slab_ref[d, 4] = (kka * e_rev).astype(bf16)
        slab_ref[d, 5] = (kmod * e_rev).astype(bf16)
        total = jnp.exp(cum_in + rev_ex)
        for c in range(tm // CHUNK):
            decay_ref[d, c] = total[c * CHUNK:c * CHUNK + 1, :]


def _rwkv_prep(conv_in, seq_len, prm):
    T = conv_in.shape[0]
    tm = TOKEN_TILE
    tiles_per_seq = seq_len // tm
    n8 = T // 8
    const = lambda shape: pl.BlockSpec(shape, lambda i: (0,) * len(shape))
    return pl.pallas_call(
        functools.partial(_rwkv_prep_body, tiles_per_seq),
        grid=(T // tm,),
        in_specs=[
            pl.BlockSpec((tm, CONV_PAD), lambda i: (i, 0)),
            pl.BlockSpec((8, CONV_PAD), lambda i: (jnp.maximum(i * (tm // 8) - 1, 0), 0)),
            pl.BlockSpec((8, CONV_PAD), lambda i: (jnp.minimum((i + 1) * (tm // 8), n8 - 1), 0)),
            const((3, CONV_PAD)), const((1, CONV_PAD)), const((1, 1024)), const((128, 1024)),
            const((1, 512)), const((128, 512)), const((1, 512)), const((1, 512)), const((1, 512)),
            const((512, 512)), const((2, 2 * tm, tm)),
        ],
        out_specs=[
            pl.BlockSpec((2, 6, tm, 512), lambda i: (0, 0, i, 0)),
            pl.BlockSpec((2, tm // CHUNK, 1, 512), lambda i: (0, i, 0, 0)),
            pl.BlockSpec((tm, 512), lambda i: (i, 0)),
            pl.BlockSpec((tm, 512), lambda i: (i, 0)),
        ],
        out_shape=[
            jax.ShapeDtypeStruct((2, 6, T, 512), bf16),
            jax.ShapeDtypeStruct((2, T // CHUNK, 1, 512), f32),
            jax.ShapeDtypeStruct((T, 512), bf16),
            jax.ShapeDtypeStruct((T, 512), f32),
        ],
        compiler_params=pltpu.CompilerParams(
            dimension_semantics=("arbitrary",), vmem_limit_bytes=VMEM_LIMIT),
        name="rwkv_prep",
    )(conv_in, conv_in, conv_in, prm["conv_w"], prm["conv_b"], prm["w0"], prm["w_up"],
      prm["a0"], prm["a_up"], prm["k_k"], prm["k_a"], prm["r_k"], prm["seg"], prm["tri"])


def _mm(a, b):
    return _dot(a.astype(bf16), b.astype(bf16))


def _stack_heads(x, head_masks):
    x = x.astype(bf16)
    return jnp.concatenate([x * hm for hm in head_masks], axis=0)


def _rwkv_scan_body(slab_f_ref, slab_b_ref, decay_f_ref, decay_b_ref, v_f_ref, v_b_ref, y_f_ref, y_b_ref,
                    z_ref):
    L = CHUNK
    n_groups = RW_WIDTH // GROUP

    @pl.when(pl.program_id(1) == 0)
    def _():
        z_ref[...] = jnp.zeros_like(z_ref)

    t_row = lax.broadcasted_iota(jnp.int32, (L, GROUP), 0)
    t_col = lax.broadcasted_iota(jnp.int32, (L, GROUP), 1) % L
    eye_cat = jnp.where(t_row == t_col, 1.0, 0.0)
    strict = [(t_col < t_row).astype(f32), (t_col > t_row).astype(f32)]
    incl = [(t_col <= t_row).astype(f32), (t_col >= t_row).astype(f32)]
    rg = lax.broadcasted_iota(jnp.int32, (GROUP, GROUP), 0)
    cg = lax.broadcasted_iota(jnp.int32, (GROUP, GROUP), 1)
    blockdiag = ((rg // HEAD_DIM) == (cg // HEAD_DIM)).astype(f32)
    eye_g = rg == cg
    lane = lax.broadcasted_iota(jnp.int32, (1, GROUP), 1) // HEAD_DIM
    head_masks = [(lane == h).astype(bf16) for h in range(HEADS_PER_GROUP)]
    stack = lambda x: _stack_heads(x, head_masks)

    chains = []
    for d, (slab_ref, decay_ref, v_ref) in enumerate(
            ((slab_f_ref, decay_f_ref, v_f_ref), (slab_b_ref, decay_b_ref, v_b_ref))):
        for sub in (range(CHUNKS_PER_STEP) if d == 0 else reversed(range(CHUNKS_PER_STEP))):
            rows = slice(sub * L, (sub + 1) * L)
            for g in range(n_groups):
                lanes = slice(g * GROUP, (g + 1) * GROUP)
                chains.append(dict(
                    d=d, g=g, lanes=lanes, rows=rows,
                    r=slab_ref[0, rows, lanes], k=slab_ref[1, rows, lanes], a=slab_ref[2, rows, lanes],
                    b=slab_ref[3, rows, lanes], bp=slab_ref[4, rows, lanes], kp=slab_ref[5, rows, lanes],
                    pdiag=decay_ref[sub, :, lanes], v=v_ref[rows, lanes]))

    for ch in chains:
        d = ch["d"]
        ch["v_s"] = stack(ch["v"])
        gram = _dot_nt(jnp.concatenate([ch["a"], ch["r"]], axis=0),
                       jnp.concatenate([stack(ch["b"]), stack(ch["k"])], axis=0))
        ch["m_ab"] = gram[:L, :GROUP] * strict[d]
        ch["m_ak"] = gram[:L, GROUP:] * strict[d]
        ch["m_rb"] = gram[L:, :GROUP] * incl[d]
        ch["m_rk"] = gram[L:, GROUP:] * incl[d]
    for ch in chains:
        ch["t"] = eye_cat + ch["m_ab"]
        ch["q"] = _mm(ch["m_ab"], stack(ch["m_ab"]))
    for _ in range(4):
        for ch in chains:
            p = _mm(jnp.concatenate([ch["t"], ch["q"]], axis=0), stack(ch["q"]))
            ch["t"] = ch["t"] + p[:L]
            ch["q"] = p[L:]
    for ch in chains:
        ch["t"] = ch["t"] + _mm(ch["t"], stack(ch["q"]))
    for ch in chains:
        ch["w1"] = _mm(ch["m_ak"], ch["v_s"])
    for ch in chains:
        au = _mm(ch["t"], jnp.concatenate([stack(ch["a"]), stack(ch["w1"])], axis=1))
        ch["a_hat"] = au[:, :GROUP]
        ch["u_hat"] = au[:, GROUP:]
    for ch in chains:
        o1 = _mm(ch["m_rb"], jnp.concatenate([stack(ch["a_hat"]), stack(ch["u_hat"])], axis=1))
        ch["r_hat"] = ch["r"].astype(f32) + o1[:, :GROUP]
        ch["y_hat"] = o1[:, GROUP:] + _mm(ch["m_rk"], ch["v_s"])
    for ch in chains:
        ch["bk_t"] = jnp.concatenate([ch["bp"].astype(f32).T, ch["kp"].astype(f32).T], axis=1).astype(bf16)
        ch["ra"] = jnp.concatenate([ch["r_hat"], ch["a_hat"]], axis=0).astype(bf16)
        ch["decay_col"] = jnp.sum(jnp.where(eye_g, ch["pdiag"], 0.0), axis=1, keepdims=True)
    y_refs = (y_f_ref, y_b_ref)
    for ch in chains:
        z = z_ref[ch["d"], ch["g"]]
        rz = _dot(ch["ra"], z.astype(bf16))
        y_refs[ch["d"]][ch["rows"], ch["lanes"]] = rz[:L] + ch["y_hat"]
        u = (rz[L:] + ch["u_hat"]).astype(bf16)
        z_ref[ch["d"], ch["g"]] = (ch["decay_col"] * z
                                   + _dot(ch["bk_t"], jnp.concatenate([u, ch["v"]], axis=0)) * blockdiag)


def _rwkv_scan(slab, decay, v, batch, seq_len):
    T = v.shape[0]
    rows = CHUNKS_PER_STEP * CHUNK
    steps = seq_len // rows
    fwd = lambda b, c: b * steps + c
    bwd = lambda b, c: b * steps + (steps - 1 - c)
    return pl.pallas_call(
        _rwkv_scan_body,
        grid=(batch, steps),
        in_specs=[
            pl.BlockSpec((None, 6, rows, 512), lambda b, c: (0, 0, fwd(b, c), 0)),
            pl.BlockSpec((None, 6, rows, 512), lambda b, c: (1, 0, bwd(b, c), 0)),
            pl.BlockSpec((None, CHUNKS_PER_STEP, 1, 512), lambda b, c: (0, fwd(b, c), 0, 0)),
            pl.BlockSpec((None, CHUNKS_PER_STEP, 1, 512), lambda b, c: (1, bwd(b, c), 0, 0)),
            pl.BlockSpec((rows, 512), lambda b, c: (fwd(b, c), 0)),
            pl.BlockSpec((rows, 512), lambda b, c: (bwd(b, c), 0)),
        ],
        out_specs=[
            pl.BlockSpec((rows, 512), lambda b, c: (fwd(b, c), 0)),
            pl.BlockSpec((rows, 512), lambda b, c: (bwd(b, c), 0)),
        ],
        out_shape=[jax.ShapeDtypeStruct((T, 512), f32), jax.ShapeDtypeStruct((T, 512), f32)],
        scratch_shapes=[pltpu.VMEM((2, RW_WIDTH // GROUP, GROUP, GROUP), f32)],
        compiler_params=pltpu.CompilerParams(
            dimension_semantics=("arbitrary", "arbitrary"), vmem_limit_bytes=VMEM_LIMIT),
        name="rwkv_scan",
    )(slab, slab, decay, decay, v, v)


def _flash_body(lam_init, tq, tk, seq_len, reach_ref, q0_ref, q1_ref, qaug_ref, k_ref, kaug_ref, v_ref,
                bias_ref, lam_ref, subln_ref, o_ref, s_ref, acc_ref):
    h = pl.program_id(1)
    qi = pl.program_id(2)
    nk = seq_len // tk
    lane_blocks = tk // DA_V_DIM

    kd = (qi * tq) // tk
    q_main = (q0_ref[...], q1_ref[...])
    q_aug = qaug_ref[...].astype(f32)
    ones_col = jnp.where(lax.broadcasted_iota(jnp.int32, (2 * tk, DA_V_DIM), 1) == 0, 1.0, 0.0).astype(bf16)

    def tile_start(kt):
        return kt * tk if isinstance(kt, int) else pl.multiple_of(kt * tk, tk)

    def scores(kt, running):
        start = tile_start(kt)
        k = jnp.concatenate([k_ref[pl.ds(start, tk), :], kaug_ref[pl.ds(start, tk), :]], axis=1)
        sign = jnp.where(kt < kd, 1.0, jnp.where(kt > kd, -1.0, 0.0)).astype(f32)
        aug = (q_aug * sign).astype(bf16)
        explicit_bias = jnp.where(kt == kd, 1.0, 0.0).astype(f32) * bias_ref[...]
        out = []
        for mi in range(2):
            s = _dot_nt(jnp.concatenate([q_main[mi], aug], axis=1), k) - explicit_bias
            s_ref[mi, kt] = s
            mx = running[mi]
            for c in range(lane_blocks):
                mx = jnp.maximum(mx, s[:, c * DA_V_DIM:(c + 1) * DA_V_DIM])
            out.append(mx)
        return tuple(out)

    span = 2 * tk
    reach = reach_ref[h]
    first_pair = jnp.maximum(qi * tq - reach, 0) // span
    n_pairs = jnp.minimum((qi * tq + tq - 1 + reach) // span + 1, nk // 2) - first_pair

    def score_pair(j, running):
        return scores(2 * j + 1, scores(2 * j, running))

    running = (jnp.full((tq, DA_V_DIM), -jnp.inf, f32),) * 2
    running = lax.fori_loop(
        0, n_pairs // 2,
        lambda i, r: score_pair(first_pair + 2 * i + 1, score_pair(first_pair + 2 * i, r)), running)
    running = lax.fori_loop(
        0, n_pairs % 2, lambda i, r: score_pair(first_pair + n_pairs - 1, r), running)
    if nk % 2:
        running = scores(nk - 1, running)
    row_max = [jnp.broadcast_to(jnp.max(mx, axis=-1, keepdims=True), (tq, DA_V_DIM)) for mx in running]

    acc_ref[...] = jnp.zeros_like(acc_ref)

    def weighted_values(mi, first, count):
        v = jnp.concatenate([v_ref[pl.ds(tile_start(first), count * tk), :], ones_col[:count * tk]], axis=1)
        ps = [jnp.exp(s_ref[mi, first + j, :, c * DA_V_DIM:(c + 1) * DA_V_DIM] - row_max[mi]).astype(bf16)
              for j in range(count) for c in range(lane_blocks)]
        return _dot(jnp.concatenate(ps, axis=1), v)

    def two_pairs(i, carry):
        j = first_pair + 2 * i
        for mi in range(2):
            acc_ref[mi] += weighted_values(mi, 2 * j, 2) + weighted_values(mi, 2 * j + 2, 2)
        return carry

    def last_pair(i, carry):
        for mi in range(2):
            acc_ref[mi] += weighted_values(mi, 2 * (first_pair + n_pairs - 1), 2)
        return carry

    lax.fori_loop(0, n_pairs // 2, two_pairs, 0)
    lax.fori_loop(0, n_pairs % 2, last_pair, 0)
    if nk % 2:
        for mi in range(2):
            acc_ref[mi] += weighted_values(mi, nk - 1, 1)

    lv = lam_ref[...]
    lam = (jnp.exp(jnp.sum(lv[0:1] * lv[1:2], axis=-1, keepdims=True))
           - jnp.exp(jnp.sum(lv[2:3] * lv[3:4], axis=-1, keepdims=True)) + lam_init)
    acc0 = acc_ref[0]
    acc1 = acc_ref[1]
    o = (acc0[:, :DA_V_DIM] / acc0[:, DA_V_DIM:DA_V_DIM + 1]
         - lam * (acc1[:, :DA_V_DIM] / acc1[:, DA_V_DIM:DA_V_DIM + 1]))
    ms = jnp.mean(o * o, axis=-1, keepdims=True)
    o_ref[...] = o * lax.rsqrt(ms + NORM_EPS) * subln_ref[...] * (1.0 - lam_init)


def _alibi_reach(q_norm, k_norm):
    gq = jnp.max(jnp.abs(q_norm))
    gk = jnp.max(jnp.abs(k_norm))
    slopes = 2.0 ** (-(ALIBI_MAX_BIAS / DA_HEADS) * jnp.arange(1, DA_HEADS + 1, dtype=f32))
    dist = (1.05 * 16.0 * gq * gk + EXP2_UNDERFLOW * math.log(2.0)) / slopes
    return jnp.minimum(jnp.ceil(dist), float(MAX_REACH)).astype(jnp.int32)


def _alibi_columns(seq_len):
    pos = jnp.arange(seq_len, dtype=jnp.int32)
    hi = (pos // POS_SPLIT).astype(f32)[None, :, None]
    lo = (pos % POS_SPLIT).astype(f32)[None, :, None]
    slopes = (2.0 ** (-(ALIBI_MAX_BIAS / DA_HEADS) * jnp.arange(1, DA_HEADS + 1, dtype=f32)))[:, None, None]
    col = jnp.arange(DA_V_DIM, dtype=jnp.int32)[None, None, :]
    q_cols = jnp.where(col == 0, -slopes * POS_SPLIT * hi,
                       jnp.where(col == 1, -slopes * lo, jnp.where(col < 4, 1.0, 0.0)))
    k_cols = jnp.where(col < 2, 1.0,
                       jnp.where(col == 2, slopes * POS_SPLIT * hi, jnp.where(col == 3, slopes * lo, 0.0)))
    return q_cols.astype(bf16), k_cols.astype(bf16)


def _flash(q0, q1, k, v, reach, lam_vec, subln, lam_init, batch, seq_len):
    T = q0.shape[0]
    tq = min(FLASH_TQ, seq_len)
    tk = min(FLASH_TK, seq_len)
    nq = seq_len // tq
    q_cols, k_cols = _alibi_columns(seq_len)
    offsets = tk // tq
    qk_dist = jnp.abs(jnp.arange(tq, dtype=jnp.int32)[None, :, None] - jnp.arange(tk, dtype=jnp.int32)[None, None, :]
                      + tq * jnp.arange(offsets, dtype=jnp.int32)[:, None, None]).astype(f32)
    slopes = 2.0 ** (-(ALIBI_MAX_BIAS / DA_HEADS) * jnp.arange(1, DA_HEADS + 1, dtype=f32))
    overlap_bias = slopes[:, None, None, None] * qk_dist[None]
    qspec = pl.BlockSpec((tq, DA_V_DIM), lambda b, h, i, reach_ref: (b * nq + i, h))
    kspec = pl.BlockSpec((seq_len, DA_V_DIM), lambda b, h, i, reach_ref: (b, h))
    return pl.pallas_call(
        functools.partial(_flash_body, lam_init, tq, tk, seq_len),
        grid_spec=pltpu.PrefetchScalarGridSpec(
            num_scalar_prefetch=1,
            grid=(batch, DA_HEADS, nq),
            in_specs=[qspec, qspec,
                      pl.BlockSpec((None, tq, DA_V_DIM), lambda b, h, i, reach_ref: (h, i, 0)),
                      kspec,
                      pl.BlockSpec((None, seq_len, DA_V_DIM), lambda b, h, i, reach_ref: (h, 0, 0)),
                      kspec,
                      pl.BlockSpec((None, None, tq, tk), lambda b, h, i, reach_ref: (h, i % offsets, 0, 0)),
                      pl.BlockSpec((4, HEAD_DIM), lambda b, h, i, reach_ref: (0, 0)),
                      pl.BlockSpec((1, DA_V_DIM), lambda b, h, i, reach_ref: (0, 0))],
            out_specs=pl.BlockSpec((tq, DA_V_DIM), lambda b, h, i, reach_ref: (b * nq + i, h)),
            scratch_shapes=[pltpu.VMEM((2, seq_len // tk, tq, tk), f32), pltpu.VMEM((2, tq, DA_EXT), f32)],
        ),
        out_shape=jax.ShapeDtypeStruct((T, DA_WIDTH), f32),
        compiler_params=pltpu.CompilerParams(
            dimension_semantics=("arbitrary", "arbitrary", "arbitrary"),
            vmem_limit_bytes=VMEM_LIMIT),
        name="flash_diff_attn",
    )(reach, q0, q1, q_cols, k, k_cols, v, overlap_bias, lam_vec, subln)


def _out_body(h_ref, yf_ref, yb_ref, bonus_ref, grw_ref, gda_ref, o_ref, p_ref, seg_ref, lng_ref, lnb_ref,
              wout_ref, pproj_ref, pnorm_ref, gw_ref, gb_ref, out_ref):
    seg = seg_ref[...]
    inv_n = 1.0 / HEAD_DIM
    y = yf_ref[...] + yb_ref[...]
    mu = _segsum(y, seg) * inv_n
    yc = y - mu
    var = _segsum(yc * yc, seg) * inv_n
    y_rw = (yc * lax.rsqrt(var + LN_X_EPS) * lng_ref[...] + lnb_ref[...] + bonus_ref[...])
    y_rw = y_rw * _silu(grw_ref[...])
    y_da = o_ref[...] * _silu(gda_ref[...])
    h1 = (h_ref[...] + _dot(y_rw.astype(bf16), wout_ref[0:512, :])
          + _dot(y_da.astype(bf16), wout_ref[512:1024, :]))
    e = _dot(p_ref[...].astype(bf16), pproj_ref[...])
    e = e * lax.rsqrt(jnp.mean(e * e, axis=-1, keepdims=True) + NORM_EPS) * pnorm_ref[...]
    gate = _sigmoid(_dot(h1.astype(bf16), gw_ref[...]) + gb_ref[...])
    out_ref[...] = h1 + gate * e


def _out_stage(h, y_f, y_b, bonus, gates, o, p, prm):
    T = h.shape[0]
    tm = TOKEN_TILE
    row = lambda w: pl.BlockSpec((tm, w), lambda i: (i, 0))
    const = lambda shape: pl.BlockSpec(shape, lambda i: (0,) * len(shape))
    return pl.pallas_call(
        _out_body,
        grid=(T // tm,),
        in_specs=[row(D_MODEL), row(512), row(512), row(512),
                  pl.BlockSpec((tm, 512), lambda i: (i, 0)),
                  pl.BlockSpec((tm, 512), lambda i: (i, 1)),
                  row(512), row(D_PLE),
                  const((512, 512)), const((1, 512)), const((1, 512)),
                  const((D_MODEL, D_MODEL)), const((D_PLE, D_MODEL)), const((1, D_MODEL)),
                  const((D_MODEL, D_MODEL)), const((1, D_MODEL))],
        out_specs=row(D_MODEL),
        out_shape=jax.ShapeDtypeStruct((T, D_MODEL), f32),
        compiler_params=pltpu.CompilerParams(
            dimension_semantics=("arbitrary",), vmem_limit_bytes=VMEM_LIMIT),
        name="out_stage",
    )(h, y_f, y_b, bonus, gates, gates, o, p, prm["seg"], prm["ln_g"], prm["ln_b"], prm["w_out"],
      prm["ple_proj"], prm["ple_norm"], prm["gate_w"], prm["gate_b"])


def _chunk_triangles(tm):
    t = np.arange(tm)
    same = (t[:, None] // CHUNK) == (t[None, :] // CHUNK)
    low_incl = same & (t[None, :] <= t[:, None])
    up_strict = same & (t[None, :] > t[:, None])
    up_incl = same & (t[None, :] >= t[:, None])
    low_strict = same & (t[None, :] < t[:, None])
    tri = np.stack([np.concatenate([low_incl, up_strict], 0), np.concatenate([up_incl, low_strict], 0)])
    return jnp.asarray(tri.astype(np.float32), dtype=bf16)


def _layer_params(i, norm_pre, w_in, w_out, rw_conv_w, rw_conv_b, rw_w0, rw_w_up, rw_a0, rw_a_up, rw_k_k,
                  rw_k_a, rw_r_k, rw_ln_g, rw_ln_b, da_q_norm, da_k_norm, da_lambda, da_subln, ple_proj,
                  ple_norm, ple_gate_w, ple_gate_b):
    pad = CONV_PAD - (CONV_COLS)
    w = w_in[i]
    o1 = CONV_COLS + RW_WIDTH
    o4 = o1 + 3 * DA_WIDTH
    w_pad = jnp.concatenate(
        [w[:, :CONV_COLS], jnp.zeros((D_MODEL, pad), f32), w[:, CONV_COLS:o1], w[:, o4:], w[:, o1:o4]],
        axis=1).astype(bf16)
    wup = rw_w_up[i]
    zeros = jnp.zeros((W_LORA, RW_WIDTH), f32)
    w_up_bd = jnp.concatenate(
        [jnp.concatenate([wup[0], zeros], 1), jnp.concatenate([zeros, wup[1]], 1)], 0).astype(bf16)
    a_up_pad = jnp.concatenate([rw_a_up[i], jnp.zeros((64, RW_WIDTH), f32)], 0).astype(bf16)
    lane = np.arange(512)
    seg = jnp.asarray((lane[:, None] // HEAD_DIM == lane[None, :] // HEAD_DIM).astype(np.float32), dtype=bf16)
    return dict(
        norm_pre=norm_pre[i].reshape(1, D_MODEL),
        w_in=w_pad,
        conv_w=jnp.pad(rw_conv_w[i], ((0, 0), (0, pad))),
        conv_b=jnp.pad(rw_conv_b[i], (0, pad)).reshape(1, CONV_PAD),
        w0=rw_w0[i].reshape(1, 2 * RW_WIDTH),
        w_up=w_up_bd,
        a0=rw_a0[i].reshape(1, RW_WIDTH),
        a_up=a_up_pad,
        k_k=rw_k_k[i].reshape(1, RW_WIDTH),
        k_a=rw_k_a[i].reshape(1, RW_WIDTH),
        r_k=rw_r_k[i].reshape(1, RW_WIDTH),
        ln_g=rw_ln_g[i].reshape(1, RW_WIDTH),
        ln_b=rw_ln_b[i].reshape(1, RW_WIDTH),
        q_norm=jnp.tile(da_q_norm[i].reshape(1, 2 * HEAD_DIM), (1, DA_HEADS)),
        k_norm=jnp.tile(da_k_norm[i].reshape(1, 2 * HEAD_DIM), (1, DA_HEADS)),
        reach=_alibi_reach(da_q_norm[i], da_k_norm[i]),
        lam_vec=da_lambda[i],
        subln=da_subln[i].reshape(1, DA_V_DIM),
        w_out=w_out[i].astype(bf16),
        ple_proj=ple_proj[i].astype(bf16),
        ple_norm=ple_norm[i].reshape(1, D_MODEL),
        gate_w=ple_gate_w[i].astype(bf16),
        gate_b=ple_gate_b[i].reshape(1, D_MODEL),
        seg=seg,
        tri=_chunk_triangles(TOKEN_TILE),
    )


def _layer(h, p, lam_init, prm, batch, seq_len):
    conv_in, gates, q0, q1, k, v = _inproj(h, prm)
    slab, decay, v_rw, bonus = _rwkv_prep(conv_in, seq_len, prm)
    y_f, y_b = _rwkv_scan(slab, decay, v_rw, batch, seq_len)
    o = _flash(q0, q1, k, v, prm["reach"], prm["lam_vec"], prm["subln"], lam_init, batch, seq_len)
    return _out_stage(h, y_f, y_b, bonus, gates, o, p, prm)


def _trunk(x, p, layers):
    batch, seq_len, _ = x.shape
    h = x.reshape(batch * seq_len, D_MODEL)
    for i, prm in enumerate(layers):
        lam_init = 0.8 - 0.6 * math.exp(-0.3 * i)
        h = _layer(h, p[i].reshape(batch * seq_len, D_PLE), lam_init, prm, batch, seq_len)
    return h.reshape(batch, seq_len, D_MODEL)


def kernel(x_prompt, x_sample, p_prompt, p_sample, norm_pre, w_in, w_out, rw_conv_w, rw_conv_b, rw_w0, rw_w_up, rw_a0, rw_a_up, rw_k_k, rw_k_a, rw_r_k, rw_ln_g, rw_ln_b, da_q_norm, da_k_norm, da_lambda, da_subln, ple_proj, ple_norm, ple_gate_w, ple_gate_b):
    depth = norm_pre.shape[0]
    layers = [_layer_params(i, norm_pre, w_in, w_out, rw_conv_w, rw_conv_b, rw_w0, rw_w_up, rw_a0, rw_a_up,
                            rw_k_k, rw_k_a, rw_r_k, rw_ln_g, rw_ln_b, da_q_norm, da_k_norm, da_lambda,
                            da_subln, ple_proj, ple_norm, ple_gate_w, ple_gate_b) for i in range(depth)]
    return (_trunk(x_prompt, p_prompt, layers), _trunk(x_sample, p_sample, layers))
```

```python
import functools
import math

import numpy as np
import jax
import jax.numpy as jnp
from jax import lax
from jax.experimental import pallas as pl
from jax.experimental.pallas import tpu as pltpu

f32 = jnp.float32
bf16 = jnp.bfloat16

D_MODEL = 1024
D_PLE = 256
RW_WIDTH = 512
HEAD_DIM = 64
W_LORA = 64
CONV_COLS = 3 * RW_WIDTH + 2 * W_LORA + 64
CONV_PAD = 1792
DA_HEADS = 4
DA_V_DIM = 128
DA_WIDTH = 512
DA_EXT = 256
GATE_COLS = 2 * 512
NORM_EPS = 1e-6
LN_X_EPS = 64e-5
KK_EPS = 1e-12
ALIBI_MAX_BIAS = 8.0
DECAY_SCALE = math.exp(-0.5)
LOG2_E = math.log2(math.e)
EXP2_UNDERFLOW = 160.0
MAX_REACH = 1 << 24
POS_SPLIT = 128

CHUNK = 64
CHUNKS_PER_STEP = 4
GROUP = 256
HEADS_PER_GROUP = GROUP // HEAD_DIM
STACK = HEADS_PER_GROUP * CHUNK
TOKEN_TILE = 256
PROJ_TILE = 512
FLASH_TQ = 512
FLASH_TK = 512
VMEM_LIMIT = 48 * 1024 * 1024
FLASH_VMEM_LIMIT = 56 * 1024 * 1024


def _dot(a, b):
    return jnp.dot(a, b, preferred_element_type=f32)


def _dot_nt(a, b):
    return lax.dot_general(a, b, (((1,), (1,)), ((), ())), preferred_element_type=f32)


def _split2(x):
    hi = x.astype(bf16)
    lo = (x - hi.astype(f32)).astype(bf16)
    return hi, lo


def _split3(x):
    hi = x.astype(bf16)
    r1 = x - hi.astype(f32)
    mid = r1.astype(bf16)
    lo = (r1 - mid.astype(f32)).astype(bf16)
    return hi, mid, lo


def _dot_exact01(m01, x, parts):
    ps = _split3(x) if parts == 3 else _split2(x)
    acc = _dot(m01, ps[0])
    for p in ps[1:]:
        acc = acc + _dot(m01, p)
    return acc


def _segsum(x, seg):
    hi, lo = _split2(x)
    return _dot(hi, seg) + _dot(lo, seg)


def _sigmoid(x):
    return 1.0 / (1.0 + jnp.exp(-x))


def _silu(x):
    return x * _sigmoid(x)


def _inproj_body(x_ref, g_ref, w_ref, qn_ref, kn_ref, seg_ref, conv_ref, gates_ref, q0_ref, q1_ref, k_ref,
                 v_ref):
    x = x_ref[...]
    ms = jnp.mean(x * x, axis=-1, keepdims=True)
    u = (x * lax.rsqrt(ms + NORM_EPS) * g_ref[...]).astype(bf16)
    for c0 in range(0, CONV_PAD, 256):
        conv_ref[:, c0:c0 + 256] = _dot(u, w_ref[:, c0:c0 + 256])
    for c0 in range(0, GATE_COLS, 256):
        gates_ref[:, c0:c0 + 256] = _dot(u, w_ref[:, CONV_PAD + c0:CONV_PAD + c0 + 256])
    base = CONV_PAD + GATE_COLS
    q = _dot(u, w_ref[:, base:base + DA_WIDTH])
    k = _dot(u, w_ref[:, base + DA_WIDTH:base + 2 * DA_WIDTH])
    v_ref[...] = _dot(u, w_ref[:, base + 2 * DA_WIDTH:base + 3 * DA_WIDTH]).astype(bf16)
    seg = seg_ref[...]
    inv_d = 1.0 / HEAD_DIM
    qn = q * lax.rsqrt(_segsum(q * q, seg) * inv_d + NORM_EPS) * qn_ref[...] * (HEAD_DIM ** -0.5)
    kn = k * lax.rsqrt(_segsum(k * k, seg) * inv_d + NORM_EPS) * kn_ref[...]
    lane = lax.broadcasted_iota(jnp.int32, (1, DA_WIDTH), 1)
    first_map = (lane // HEAD_DIM) % 2 == 0
    q0_ref[...] = jnp.where(first_map, qn, 0.0).astype(bf16)
    q1_ref[...] = jnp.where(first_map, 0.0, qn).astype(bf16)
    k_ref[...] = kn.astype(bf16)


def _inproj(h, prm):
    T = h.shape[0]
    tm = PROJ_TILE
    row = lambda w: pl.BlockSpec((tm, w), lambda i: (i, 0))
    const = lambda shape: pl.BlockSpec(shape, lambda i: (0,) * len(shape))
    attn = jax.ShapeDtypeStruct((T, DA_WIDTH), bf16)
    return pl.pallas_call(
        _inproj_body,
        grid=(T // tm,),
        in_specs=[row(D_MODEL), const((1, D_MODEL)), const((D_MODEL, CONV_PAD + GATE_COLS + 3 * DA_WIDTH)),
                  const((1, DA_WIDTH)), const((1, DA_WIDTH)), const((512, 512))],
        out_specs=[row(CONV_PAD), row(GATE_COLS), row(DA_WIDTH), row(DA_WIDTH), row(DA_WIDTH), row(DA_WIDTH)],
        out_shape=[jax.ShapeDtypeStruct((T, CONV_PAD), f32), jax.ShapeDtypeStruct((T, GATE_COLS), f32),
                   attn, attn, attn, attn],
        compiler_params=pltpu.CompilerParams(
            dimension_semantics=("arbitrary",), vmem_limit_bytes=VMEM_LIMIT),
        name="inproj",
    )(h, prm["norm_pre"], prm["w_in"], prm["q_norm"], prm["k_norm"], prm["seg"])


def _rwkv_prep_body(tiles_per_seq, main_ref, prev_ref, next_ref, cw_ref, cb_ref, w0_ref, wup_ref,
                    a0_ref, aup_ref, kk_ref, ka_ref, rk_ref, seg_ref, tri_ref,
                    slab_ref, decay_ref, v_ref, bonus_ref):
    tm = main_ref.shape[0]
    i = pl.program_id(0)
    local = i % tiles_per_seq
    is_first = local == 0
    is_last = local == tiles_per_seq - 1

    xm = main_ref[...]
    row = lax.broadcasted_iota(jnp.int32, (tm, 1), 0)
    prev_row = jnp.where(is_first, 0.0, prev_ref[7:8, :])
    next_row = jnp.where(is_last, 0.0, next_ref[0:1, :])
    xp = jnp.where(row == 0, prev_row, pltpu.roll(xm, 1, 0))
    xn = jnp.where(row == tm - 1, next_row, pltpu.roll(xm, tm - 1, 0))
    cw = cw_ref[...]
    c = cb_ref[...] + xp * cw[0:1] + xm * cw[1:2] + xn * cw[2:3]

    r = c[:, 0:512]
    k = c[:, 512:1024]
    v = c[:, 1024:1536]
    zw = c[:, 1536:1664]
    za = c[:, 1664:1792]

    wl = w0_ref[...] + _dot(jnp.tanh(zw).astype(bf16), wup_ref[...])
    logw = -DECAY_SCALE * _sigmoid(wl)
    a = _sigmoid(a0_ref[...] + _dot(za.astype(bf16), aup_ref[...]))
    seg = seg_ref[...]
    kk = k * kk_ref[...]
    kk = kk * lax.rsqrt(_segsum(kk * kk, seg) + KK_EPS)
    kmod = k * (1.0 + (a - 1.0) * ka_ref[...])
    kka = kk * a
    bonus_ref[...] = _segsum(r * kmod * rk_ref[...], seg) * v
    v_ref[...] = v.astype(bf16)

    for d in range(2):
        lw = logw[:, d * 512:(d + 1) * 512]
        cums = _dot_exact01(tri_ref[d], lw, 2)
        cum_in = cums[:tm]
        rev_ex = cums[tm:]
        e_pos = jnp.exp(cum_in)
        e_neg = jnp.exp(-cum_in)
        e_rev = jnp.exp(rev_ex)
        slab_ref[d, 0] = (r * e_pos).astype(bf16)
        slab_ref[d, 1] = (kmod * e_neg).astype(bf16)
        slab_ref[d, 2] = (-kk * jnp.exp(cum_in - lw)).astype(bf16)
        slab_ref[d, 3] = (kka * e_neg).astype(bf16)
        slab_ref[d, 4] = (kka * e_rev).astype(bf16)
        slab_ref[d, 5] = (kmod * e_rev).astype(bf16)
        total = jnp.exp(cum_in + rev_ex)
        for c in range(tm // CHUNK):
            decay_ref[d, c] = total[c * CHUNK:c * CHUNK + 1, :]


def _rwkv_prep(conv_in, seq_len, prm):
    T = conv_in.shape[0]
    tm = TOKEN_TILE
    tiles_per_seq = seq_len // tm
    n8 = T // 8
    const = lambda shape: pl.BlockSpec(shape, lambda i: (0,) * len(shape))
    return pl.pallas_call(
        functools.partial(_rwkv_prep_body, tiles_per_seq),
        grid=(T // tm,),
        in_specs=[
            pl.BlockSpec((tm, CONV_PAD), lambda i: (i, 0)),
            pl.BlockSpec((8, CONV_PAD), lambda i: (jnp.maximum(i * (tm // 8) - 1, 0), 0)),
            pl.BlockSpec((8, CONV_PAD), lambda i: (jnp.minimum((i + 1) * (tm // 8), n8 - 1), 0)),
            const((3, CONV_PAD)), const((1, CONV_PAD)), const((1, 1024)), const((128, 1024)),
            const((1, 512)), const((128, 512)), const((1, 512)), const((1, 512)), const((1, 512)),
            const((512, 512)), const((2, 2 * tm, tm)),
        ],
        out_specs=[
            pl.BlockSpec((2, 6, tm, 512), lambda i: (0, 0, i, 0)),
            pl.BlockSpec((2, tm // CHUNK, 1, 512), lambda i: (0, i, 0, 0)),
            pl.BlockSpec((tm, 512), lambda i: (i, 0)),
            pl.BlockSpec((tm, 512), lambda i: (i, 0)),
        ],
        out_shape=[
            jax.ShapeDtypeStruct((2, 6, T, 512), bf16),
            jax.ShapeDtypeStruct((2, T // CHUNK, 1, 512), f32),
            jax.ShapeDtypeStruct((T, 512), bf16),
            jax.ShapeDtypeStruct((T, 512), f32),
        ],
        compiler_params=pltpu.CompilerParams(
            dimension_semantics=("arbitrary",), vmem_limit_bytes=VMEM_LIMIT),
        name="rwkv_prep",
    )(conv_in, conv_in, conv_in, prm["conv_w"], prm["conv_b"], prm["w0"], prm["w_up"],
      prm["a0"], prm["a_up"], prm["k_k"], prm["k_a"], prm["r_k"], prm["seg"], prm["tri"])


def _mm(a, b):
    return _dot(a.astype(bf16), b.astype(bf16))


def _stack_heads(x, head_masks):
    x = x.astype(bf16)
    return jnp.concatenate([x * hm for hm in head_masks], axis=0)


def _rwkv_scan_body(slab_f_ref, slab_b_ref, decay_f_ref, decay_b_ref, v_f_ref, v_b_ref, y_f_ref, y_b_ref,
                    z_ref):
    L = CHUNK
    n_groups = RW_WIDTH // GROUP

    @pl.when(pl.program_id(1) == 0)
    def _():
        z_ref[...] = jnp.zeros_like(z_ref)

    t_row = lax.broadcasted_iota(jnp.int32, (L, GROUP), 0)
    t_col = lax.broadcasted_iota(jnp.int32, (L, GROUP), 1) % L
    eye_cat = jnp.where(t_row == t_col, 1.0, 0.0)
    strict = [(t_col < t_row).astype(f32), (t_col > t_row).astype(f32)]
    incl = [(t_col <= t_row).astype(f32), (t_col >= t_row).astype(f32)]
    rg = lax.broadcasted_iota(jnp.int32, (GROUP, GROUP), 0)
    cg = lax.broadcasted_iota(jnp.int32, (GROUP, GROUP), 1)
    blockdiag = ((rg // HEAD_DIM) == (cg // HEAD_DIM)).astype(f32)
    eye_g = rg == cg
    lane = lax.broadcasted_iota(jnp.int32, (1, GROUP), 1) // HEAD_DIM
    head_masks = [(lane == h).astype(bf16) for h in range(HEADS_PER_GROUP)]
    stack = lambda x: _stack_heads(x, head_masks)

    chains = []
    for d, (slab_ref, decay_ref, v_ref) in enumerate(
            ((slab_f_ref, decay_f_ref, v_f_ref), (slab_b_ref, decay_b_ref, v_b_ref))):
        for sub in (range(CHUNKS_PER_STEP) if d == 0 else reversed(range(CHUNKS_PER_STEP))):
            rows = slice(sub * L, (sub + 1) * L)
            for g in range(n_groups):
                lanes = slice(g * GROUP, (g + 1) * GROUP)
                chains.append(dict(
                    d=d, g=g, lanes=lanes, rows=rows,
                    r=slab_ref[0, rows, lanes], k=slab_ref[1, rows, lanes], a=slab_ref[2, rows, lanes],
                    b=slab_ref[3, rows, lanes], bp=slab_ref[4, rows, lanes], kp=slab_ref[5, rows, lanes],
                    pdiag=decay_ref[sub, :, lanes], v=v_ref[rows, lanes]))

    for ch in chains:
        d = ch["d"]
        ch["v_s"] = stack(ch["v"])
        gram = _dot_nt(jnp.concatenate([ch["a"], ch["r"]], axis=0),
                       jnp.concatenate([stack(ch["b"]), stack(ch["k"])], axis=0))
        ch["m_ab"] = gram[:L, :GROUP] * strict[d]
        ch["m_ak"] = gram[:L, GROUP:] * strict[d]
        ch["m_rb"] = gram[L:, :GROUP] * incl[d]
        ch["m_rk"] = gram[L:, GROUP:] * incl[d]
    for ch in chains:
        ch["t"] = eye_cat + ch["m_ab"]
        ch["q"] = _mm(ch["m_ab"], stack(ch["m_ab"]))
    for _ in range(4):
        for ch in chains:
            p = _mm(jnp.concatenate([ch["t"], ch["q"]], axis=0), stack(ch["q"]))
            ch["t"] = ch["t"] + p[:L]
            ch["q"] = p[L:]
    for ch in chains:
        ch["t"] = ch["t"] + _mm(ch["t"], stack(ch["q"]))
    for ch in chains:
        ch["w1"] = _mm(ch["m_ak"], ch["v_s"])
    for ch in chains:
        au = _mm(ch["t"], jnp.concatenate([stack(ch["a"]), stack(ch["w1"])], axis=1))
        ch["a_hat"] = au[:, :GROUP]
        ch["u_hat"] = au[:, GROUP:]
    for ch in chains:
        o1 = _mm(ch["m_rb"], jnp.concatenate([stack(ch["a_hat"]), stack(ch["u_hat"])], axis=1))
        ch["r_hat"] = ch["r"].astype(f32) + o1[:, :GROUP]
        ch["y_hat"] = o1[:, GROUP:] + _mm(ch["m_rk"], ch["v_s"])
    for ch in chains:
        ch["bk_t"] = jnp.concatenate([ch["bp"].astype(f32).T, ch["kp"].astype(f32).T], axis=1).astype(bf16)
        ch["ra"] = jnp.concatenate([ch["r_hat"], ch["a_hat"]], axis=0).astype(bf16)
        ch["decay_col"] = jnp.sum(jnp.where(eye_g, ch["pdiag"], 0.0), axis=1, keepdims=True)
    y_refs = (y_f_ref, y_b_ref)
    for ch in chains:
        z = z_ref[ch["d"], ch["g"]]
        rz = _dot(ch["ra"], z.astype(bf16))
        y_refs[ch["d"]][ch["rows"], ch["lanes"]] = rz[:L] + ch["y_hat"]
        u = (rz[L:] + ch["u_hat"]).astype(bf16)
        z_ref[ch["d"], ch["g"]] = (ch["decay_col"] * z
                                   + _dot(ch["bk_t"], jnp.concatenate([u, ch["v"]], axis=0)) * blockdiag)


def _rwkv_scan(slab, decay, v, batch, seq_len):
    T = v.shape[0]
    rows = CHUNKS_PER_STEP * CHUNK
    steps = seq_len // rows
    fwd = lambda b, c: b * steps + c
    bwd = lambda b, c: b * steps + (steps - 1 - c)
    return pl.pallas_call(
        _rwkv_scan_body,
        grid=(batch, steps),
        in_specs=[
            pl.BlockSpec((None, 6, rows, 512), lambda b, c: (0, 0, fwd(b, c), 0)),
            pl.BlockSpec((None, 6, rows, 512), lambda b, c: (1, 0, bwd(b, c), 0)),
            pl.BlockSpec((None, CHUNKS_PER_STEP, 1, 512), lambda b, c: (0, fwd(b, c), 0, 0)),
            pl.BlockSpec((None, CHUNKS_PER_STEP, 1, 512), lambda b, c: (1, bwd(b, c), 0, 0)),
            pl.BlockSpec((rows, 512), lambda b, c: (fwd(b, c), 0)),
            pl.BlockSpec((rows, 512), lambda b, c: (bwd(b, c), 0)),
        ],
        out_specs=[
            pl.BlockSpec((rows, 512), lambda b, c: (fwd(b, c), 0)),
            pl.BlockSpec((rows, 512), lambda b, c: (bwd(b, c), 0)),
        ],
        out_shape=[jax.ShapeDtypeStruct((T, 512), f32), jax.ShapeDtypeStruct((T, 512), f32)],
        scratch_shapes=[pltpu.VMEM((2, RW_WIDTH // GROUP, GROUP, GROUP), f32)],
        compiler_params=pltpu.CompilerParams(
            dimension_semantics=("arbitrary", "arbitrary"), vmem_limit_bytes=VMEM_LIMIT),
        name="rwkv_scan",
    )(slab, slab, decay, decay, v, v)


def _flash_body(lam_init, tq, tk, seq_len, reach_ref, q0_ref, q1_ref, qaug_ref, k_ref, kaug_ref, v_ref,
                bias_ref, lam_ref, subln_ref, o_ref, s_ref, acc_ref):
    h = pl.program_id(1)
    qi = pl.program_id(2)
    nk = seq_len // tk
    lane_blocks = tk // DA_V_DIM

    kd = (qi * tq) // tk
    q_main = (q0_ref[...], q1_ref[...])
    q_aug = qaug_ref[...].astype(f32)
    ones_col = jnp.where(lax.broadcasted_iota(jnp.int32, (2 * tk, DA_V_DIM), 1) == 0, 1.0, 0.0).astype(bf16)

    def tile_start(kt):
        return kt * tk if isinstance(kt, int) else pl.multiple_of(kt * tk, tk)

    def scores(kt, running):
        start = tile_start(kt)
        k = jnp.concatenate([k_ref[pl.ds(start, tk), :], kaug_ref[pl.ds(start, tk), :]], axis=1)
        sign = jnp.where(kt < kd, 1.0, jnp.where(kt > kd, -1.0, 0.0)).astype(f32)
        aug = (q_aug * sign).astype(bf16)
        explicit_bias = jnp.where(kt == kd, 1.0, 0.0).astype(f32) * bias_ref[...]
        out = []
        for mi in range(2):
            s = _dot_nt(jnp.concatenate([q_main[mi], aug], axis=1), k) - explicit_bias
            s_ref[mi, kt] = s
            mx = running[mi]
            for c in range(lane_blocks):
                mx = jnp.maximum(mx, s[:, c * DA_V_DIM:(c + 1) * DA_V_DIM])
            out.append(mx)
        return tuple(out)

    span = 2 * tk
    reach = reach_ref[h]
    first_pair = jnp.maximum(qi * tq - reach, 0) // span
    n_pairs = jnp.minimum((qi * tq + tq - 1 + reach) // span + 1, nk // 2) - first_pair

    def score_pair(j, running):
        return scores(2 * j + 1, scores(2 * j, running))

    running = (jnp.full((tq, DA_V_DIM), -jnp.inf, f32),) * 2
    running = lax.fori_loop(
        0, n_pairs // 2,
        lambda i, r: score_pair(first_pair + 2 * i + 1, score_pair(first_pair + 2 * i, r)), running)
    running = lax.fori_loop(
        0, n_pairs % 2, lambda i, r: score_pair(first_pair + n_pairs - 1, r), running)
    if nk % 2:
        running = scores(nk - 1, running)
    row_max = [jnp.broadcast_to(jnp.max(mx, axis=-1, keepdims=True), (tq, DA_V_DIM)) for mx in running]

    acc_ref[...] = jnp.zeros_like(acc_ref)

    def weighted_values(mi, first, count):
        v = jnp.concatenate([v_ref[pl.ds(tile_start(first), count * tk), :], ones_col[:count * tk]], axis=1)
        ps = [jnp.exp(s_ref[mi, first + j, :, c * DA_V_DIM:(c + 1) * DA_V_DIM] - row_max[mi]).astype(bf16)
              for j in range(count) for c in range(lane_blocks)]
        return _dot(jnp.concatenate(ps, axis=1), v)

    def two_pairs(i, carry):
        j = first_pair + 2 * i
        for mi in range(2):
            acc_ref[mi] += weighted_values(mi, 2 * j, 2) + weighted_values(mi, 2 * j + 2, 2)
        return carry

    def last_pair(i, carry):
        for mi in range(2):
            acc_ref[mi] += weighted_values(mi, 2 * (first_pair + n_pairs - 1), 2)
        return carry

    lax.fori_loop(0, n_pairs // 2, two_pairs, 0)
    lax.fori_loop(0, n_pairs % 2, last_pair, 0)
    if nk % 2:
        for mi in range(2):
            acc_ref[mi] += weighted_values(mi, nk - 1, 1)

    lv = lam_ref[...]
    lam = (jnp.exp(jnp.sum(lv[0:1] * lv[1:2], axis=-1, keepdims=True))
           - jnp.exp(jnp.sum(lv[2:3] * lv[3:4], axis=-1, keepdims=True)) + lam_init)
    acc0 = acc_ref[0]
    acc1 = acc_ref[1]
    o = (acc0[:, :DA_V_DIM] / acc0[:, DA_V_DIM:DA_V_DIM + 1]
         - lam * (acc1[:, :DA_V_DIM] / acc1[:, DA_V_DIM:DA_V_DIM + 1]))
    ms = jnp.mean(o * o, axis=-1, keepdims=True)
    o_ref[...] = o * lax.rsqrt(ms + NORM_EPS) * subln_ref[...] * (1.0 - lam_init)


def _alibi_reach(q_norm, k_norm):
    gq = jnp.max(jnp.abs(q_norm))
    gk = jnp.max(jnp.abs(k_norm))
    slopes = 2.0 ** (-(ALIBI_MAX_BIAS / DA_HEADS) * jnp.arange(1, DA_HEADS + 1, dtype=f32))
    dist = (1.05 * 16.0 * gq * gk + EXP2_UNDERFLOW * math.log(2.0)) / slopes
    return jnp.minimum(jnp.ceil(dist), float(MAX_REACH)).astype(jnp.int32)


def _alibi_columns(seq_len):
    pos = jnp.arange(seq_len, dtype=jnp.int32)
    hi = (pos // POS_SPLIT).astype(f32)[None, :, None]
    lo = (pos % POS_SPLIT).astype(f32)[None, :, None]
    slopes = (2.0 ** (-(ALIBI_MAX_BIAS / DA_HEADS) * jnp.arange(1, DA_HEADS + 1, dtype=f32)))[:, None, None]
    col = jnp.arange(DA_V_DIM, dtype=jnp.int32)[None, None, :]
    q_cols = jnp.where(col == 0, -slopes * POS_SPLIT * hi,
                       jnp.where(col == 1, -slopes * lo, jnp.where(col < 4, 1.0, 0.0)))
    k_cols = jnp.where(col < 2, 1.0,
                       jnp.where(col == 2, slopes * POS_SPLIT * hi, jnp.where(col == 3, slopes * lo, 0.0)))
    return q_cols.astype(bf16), k_cols.astype(bf16)


def _flash(q0, q1, k, v, reach, lam_vec, subln, lam_init, batch, seq_len):
    T = q0.shape[0]
    tq = min(FLASH_TQ, seq_len)
    tk = min(FLASH_TK, seq_len)
    nq = seq_len // tq
    q_cols, k_cols = _alibi_columns(seq_len)
    offsets = tk // tq
    qk_dist = jnp.abs(jnp.arange(tq, dtype=jnp.int32)[None, :, None] - jnp.arange(tk, dtype=jnp.int32)[None, None, :]
                      + tq * jnp.arange(offsets, dtype=jnp.int32)[:, None, None]).astype(f32)
    slopes = 2.0 ** (-(ALIBI_MAX_BIAS / DA_HEADS) * jnp.arange(1, DA_HEADS + 1, dtype=f32))
    overlap_bias = slopes[:, None, None, None] * qk_dist[None]
    qspec = pl.BlockSpec((tq, DA_V_DIM), lambda b, h, i, reach_ref: (b * nq + i, h))
    kspec = pl.BlockSpec((seq_len, DA_V_DIM), lambda b, h, i, reach_ref: (b, h), pipeline_mode=pl.Buffered(1))
    return pl.pallas_call(
        functools.partial(_flash_body, lam_init, tq, tk, seq_len),
        grid_spec=pltpu.PrefetchScalarGridSpec(
            num_scalar_prefetch=1,
            grid=(batch, DA_HEADS, nq),
            in_specs=[qspec, qspec,
                      pl.BlockSpec((None, tq, DA_V_DIM), lambda b, h, i, reach_ref: (h, i, 0)),
                      kspec,
                      pl.BlockSpec((None, seq_len, DA_V_DIM), lambda b, h, i, reach_ref: (h, 0, 0),
                                   pipeline_mode=pl.Buffered(1)),
                      kspec,
                      pl.BlockSpec((None, None, tq, tk), lambda b, h, i, reach_ref: (h, i % offsets, 0, 0)),
                      pl.BlockSpec((4, HEAD_DIM), lambda b, h, i, reach_ref: (0, 0)),
                      pl.BlockSpec((1, DA_V_DIM), lambda b, h, i, reach_ref: (0, 0))],
            out_specs=pl.BlockSpec((tq, DA_V_DIM), lambda b, h, i, reach_ref: (b * nq + i, h)),
            scratch_shapes=[pltpu.VMEM((2, seq_len // tk, tq, tk), f32), pltpu.VMEM((2, tq, DA_EXT), f32)],
        ),
        out_shape=jax.ShapeDtypeStruct((T, DA_WIDTH), f32),
        compiler_params=pltpu.CompilerParams(
            dimension_semantics=("arbitrary", "arbitrary", "arbitrary"),
            vmem_limit_bytes=FLASH_VMEM_LIMIT),
        name="flash_diff_attn",
    )(reach, q0, q1, q_cols, k, k_cols, v, overlap_bias, lam_vec, subln)


def _out_body(h_ref, yf_ref, yb_ref, bonus_ref, grw_ref, gda_ref, o_ref, p_ref, seg_ref, lng_ref, lnb_ref,
              wout_ref, pproj_ref, pnorm_ref, gw_ref, gb_ref, out_ref):
    seg = seg_ref[...]
    inv_n = 1.0 / HEAD_DIM
    y = yf_ref[...] + yb_ref[...]
    mu = _segsum(y, seg) * inv_n
    yc = y - mu
    var = _segsum(yc * yc, seg) * inv_n
    y_rw = (yc * lax.rsqrt(var + LN_X_EPS) * lng_ref[...] + lnb_ref[...] + bonus_ref[...])
    y_rw = y_rw * _silu(grw_ref[...])
    y_da = o_ref[...] * _silu(gda_ref[...])
    h1 = (h_ref[...] + _dot(y_rw.astype(bf16), wout_ref[0:512, :])
          + _dot(y_da.astype(bf16), wout_ref[512:1024, :]))
    e = _dot(p_ref[...].astype(bf16), pproj_ref[...])
    e = e * lax.rsqrt(jnp.mean(e * e, axis=-1, keepdims=True) + NORM_EPS) * pnorm_ref[...]
    gate = _sigmoid(_dot(h1.astype(bf16), gw_ref[...]) + gb_ref[...])
    out_ref[...] = h1 + gate * e


def _out_stage(h, y_f, y_b, bonus, gates, o, p, prm):
    T = h.shape[0]
    tm = PROJ_TILE
    row = lambda w: pl.BlockSpec((tm, w), lambda i: (i, 0))
    const = lambda shape: pl.BlockSpec(shape, lambda i: (0,) * len(shape))
    return pl.pallas_call(
        _out_body,
        grid=(T // tm,),
        in_specs=[row(D_MODEL), row(512), row(512), row(512),
                  pl.BlockSpec((tm, 512), lambda i: (i, 0)),
                  pl.BlockSpec((tm, 512), lambda i: (i, 1)),
                  row(512), row(D_PLE),
                  const((512, 512)), const((1, 512)), const((1, 512)),
                  const((D_MODEL, D_MODEL)), const((D_PLE, D_MODEL)), const((1, D_MODEL)),
                  const((D_MODEL, D_MODEL)), const((1, D_MODEL))],
        out_specs=row(D_MODEL),
        out_shape=jax.ShapeDtypeStruct((T, D_MODEL), f32),
        compiler_params=pltpu.CompilerParams(
            dimension_semantics=("arbitrary",), vmem_limit_bytes=VMEM_LIMIT),
        name="out_stage",
    )(h, y_f, y_b, bonus, gates, gates, o, p, prm["seg"], prm["ln_g"], prm["ln_b"], prm["w_out"],
      prm["ple_proj"], prm["ple_norm"], prm["gate_w"], prm["gate_b"])


def _chunk_triangles(tm):
    t = np.arange(tm)
    same = (t[:, None] // CHUNK) == (t[None, :] // CHUNK)
    low_incl = same & (t[None, :] <= t[:, None])
    up_strict = same & (t[None, :] > t[:, None])
    up_incl = same & (t[None, :] >= t[:, None])
    low_strict = same & (t[None, :] < t[:, None])
    tri = np.stack([np.concatenate([low_incl, up_strict], 0), np.concatenate([up_incl, low_strict], 0)])
    return jnp.asarray(tri.astype(np.float32), dtype=bf16)


def _layer_params(i, norm_pre, w_in, w_out, rw_conv_w, rw_conv_b, rw_w0, rw_w_up, rw_a0, rw_a_up, rw_k_k,
                  rw_k_a, rw_r_k, rw_ln_g, rw_ln_b, da_q_norm, da_k_norm, da_lambda, da_subln, ple_proj,
                  ple_norm, ple_gate_w, ple_gate_b):
    pad = CONV_PAD - (CONV_COLS)
    w = w_in[i]
    o1 = CONV_COLS + RW_WIDTH
    o4 = o1 + 3 * DA_WIDTH
    w_pad = jnp.concatenate(
        [w[:, :CONV_COLS], jnp.zeros((D_MODEL, pad), f32), w[:, CONV_COLS:o1], w[:, o4:], w[:, o1:o4]],
        axis=1).astype(bf16)
    wup = rw_w_up[i]
    zeros = jnp.zeros((W_LORA, RW_WIDTH), f32)
    w_up_bd = jnp.concatenate(
        [jnp.concatenate([wup[0], zeros], 1), jnp.concatenate([zeros, wup[1]], 1)], 0).astype(bf16)
    a_up_pad = jnp.concatenate([rw_a_up[i], jnp.zeros((64, RW_WIDTH), f32)], 0).astype(bf16)
    lane = np.arange(512)
    seg = jnp.asarray((lane[:, None] // HEAD_DIM == lane[None, :] // HEAD_DIM).astype(np.float32), dtype=bf16)
    return dict(
        norm_pre=norm_pre[i].reshape(1, D_MODEL),
        w_in=w_pad,
        conv_w=jnp.pad(rw_conv_w[i], ((0, 0), (0, pad))),
        conv_b=jnp.pad(rw_conv_b[i], (0, pad)).reshape(1, CONV_PAD),
        w0=rw_w0[i].reshape(1, 2 * RW_WIDTH),
        w_up=w_up_bd,
        a0=rw_a0[i].reshape(1, RW_WIDTH),
        a_up=a_up_pad,
        k_k=rw_k_k[i].reshape(1, RW_WIDTH),
        k_a=rw_k_a[i].reshape(1, RW_WIDTH),
        r_k=rw_r_k[i].reshape(1, RW_WIDTH),
        ln_g=rw_ln_g[i].reshape(1, RW_WIDTH),
        ln_b=rw_ln_b[i].reshape(1, RW_WIDTH),
        q_norm=jnp.tile(da_q_norm[i].reshape(1, 2 * HEAD_DIM), (1, DA_HEADS)),
        k_norm=jnp.tile(da_k_norm[i].reshape(1, 2 * HEAD_DIM), (1, DA_HEADS)),
        reach=_alibi_reach(da_q_norm[i], da_k_norm[i]),
        lam_vec=da_lambda[i],
        subln=da_subln[i].reshape(1, DA_V_DIM),
        w_out=w_out[i].astype(bf16),
        ple_proj=ple_proj[i].astype(bf16),
        ple_norm=ple_norm[i].reshape(1, D_MODEL),
        gate_w=ple_gate_w[i].astype(bf16),
        gate_b=ple_gate_b[i].reshape(1, D_MODEL),
        seg=seg,
        tri=_chunk_triangles(TOKEN_TILE),
    )


def _layer(h, p, lam_init, prm, batch, seq_len):
    conv_in, gates, q0, q1, k, v = _inproj(h, prm)
    slab, decay, v_rw, bonus = _rwkv_prep(conv_in, seq_len, prm)
    y_f, y_b = _rwkv_scan(slab, decay, v_rw, batch, seq_len)
    o = _flash(q0, q1, k, v, prm["reach"], prm["lam_vec"], prm["subln"], lam_init, batch, seq_len)
    return _out_stage(h, y_f, y_b, bonus, gates, o, p, prm)


def _trunk(x, p, layers):
    batch, seq_len, _ = x.shape
    h = x.reshape(batch * seq_len, D_MODEL)
    for i, prm in enumerate(layers):
        lam_init = 0.8 - 0.6 * math.exp(-0.3 * i)
        h = _layer(h, p[i].reshape(batch * seq_len, D_PLE), lam_init, prm, batch, seq_len)
    return h.reshape(batch, seq_len, D_MODEL)


def kernel(x_prompt, x_sample, p_prompt, p_sample, norm_pre, w_in, w_out, rw_conv_w, rw_conv_b, rw_w0, rw_w_up, rw_a0, rw_a_up, rw_k_k, rw_k_a, rw_r_k, rw_ln_g, rw_ln_b, da_q_norm, da_k_norm, da_lambda, da_subln, ple_proj, ple_norm, ple_gate_w, ple_gate_b):
    depth = norm_pre.shape[0]
    layers = [_layer_params(i, norm_pre, w_in, w_out, rw_conv_w, rw_conv_b, rw_w0, rw_w_up, rw_a0, rw_a_up,
                            rw_k_k, rw_k_a, rw_r_k, rw_ln_g, rw_ln_b, da_q_norm, da_k_norm, da_lambda,
                            da_subln, ple_proj, ple_norm, ple_gate_w, ple_gate_b) for i in range(depth)]
    return (_trunk(x_prompt, p_prompt, layers), _trunk(x_sample, p_sample, layers))
```

```python
import functools
import math

import numpy as np
import jax
import jax.numpy as jnp
from jax import lax
from jax.experimental import pallas as pl
from jax.experimental.pallas import tpu as pltpu

f32 = jnp.float32
bf16 = jnp.bfloat16

D_MODEL = 1024
D_PLE = 256
RW_WIDTH = 512
HEAD_DIM = 64
W_LORA = 64
CONV_COLS = 3 * RW_WIDTH + 2 * W_LORA + 64
CONV_PAD = 1792
DA_HEADS = 4
DA_V_DIM = 128
DA_WIDTH = 512
DA_EXT = 256
GATE_COLS = 2 * 512
NORM_EPS = 1e-6
LN_X_EPS = 64e-5
KK_EPS = 1e-12
ALIBI_MAX_BIAS = 8.0
DECAY_SCALE = math.exp(-0.5)
EXP2_UNDERFLOW = 160.0
MAX_REACH = 1 << 24
POS_SPLIT = 128

CHUNK = 64
CHUNKS_PER_STEP = 4
GROUP = 256
HEADS_PER_GROUP = GROUP // HEAD_DIM
TOKEN_TILE = 256
PROJ_TILE = 512
FLASH_TQ = 512
FLASH_TK = 512
VMEM_LIMIT = 48 * 1024 * 1024
FLASH_VMEM_LIMIT = 56 * 1024 * 1024


def _dot(a, b):
    return jnp.dot(a, b, preferred_element_type=f32)


def _dot_nt(a, b):
    return lax.dot_general(a, b, (((1,), (1,)), ((), ())), preferred_element_type=f32)


def _split2(x):
    hi = x.astype(bf16)
    lo = (x - hi.astype(f32)).astype(bf16)
    return hi, lo


def _split3(x):
    hi = x.astype(bf16)
    r1 = x - hi.astype(f32)
    mid = r1.astype(bf16)
    lo = (r1 - mid.astype(f32)).astype(bf16)
    return hi, mid, lo


def _dot_exact01(m01, x, parts):
    ps = _split3(x) if parts == 3 else _split2(x)
    acc = _dot(m01, ps[0])
    for p in ps[1:]:
        acc = acc + _dot(m01, p)
    return acc


def _segsum(x, seg):
    hi, lo = _split2(x)
    return _dot(hi, seg) + _dot(lo, seg)


def _sigmoid(x):
    return 1.0 / (1.0 + jnp.exp(-x))


def _silu(x):
    return x * _sigmoid(x)


def _inproj_body(x_ref, g_ref, w_ref, qn_ref, kn_ref, seg_ref, conv_ref, gates_ref, q0_ref, q1_ref, k_ref,
                 v_ref):
    x = x_ref[...]
    ms = jnp.mean(x * x, axis=-1, keepdims=True)
    u = (x * lax.rsqrt(ms + NORM_EPS) * g_ref[...]).astype(bf16)
    for c0 in range(0, CONV_PAD, 256):
        conv_ref[:, c0:c0 + 256] = _dot(u, w_ref[:, c0:c0 + 256])
    for c0 in range(0, GATE_COLS, 256):
        gates_ref[:, c0:c0 + 256] = _dot(u, w_ref[:, CONV_PAD + c0:CONV_PAD + c0 + 256])
    base = CONV_PAD + GATE_COLS
    q = _dot(u, w_ref[:, base:base + DA_WIDTH])
    k = _dot(u, w_ref[:, base + DA_WIDTH:base + 2 * DA_WIDTH])
    v_ref[...] = _dot(u, w_ref[:, base + 2 * DA_WIDTH:base + 3 * DA_WIDTH]).astype(bf16)
    seg = seg_ref[...]
    inv_d = 1.0 / HEAD_DIM
    qn = q * lax.rsqrt(_segsum(q * q, seg) * inv_d + NORM_EPS) * qn_ref[...] * (HEAD_DIM ** -0.5)
    kn = k * lax.rsqrt(_segsum(k * k, seg) * inv_d + NORM_EPS) * kn_ref[...]
    lane = lax.broadcasted_iota(jnp.int32, (1, DA_WIDTH), 1)
    first_map = (lane // HEAD_DIM) % 2 == 0
    q0_ref[...] = jnp.where(first_map, qn, 0.0).astype(bf16)
    q1_ref[...] = jnp.where(first_map, 0.0, qn).astype(bf16)
    k_ref[...] = kn.astype(bf16)


def _inproj(h, prm):
    T = h.shape[0]
    tm = PROJ_TILE
    row = lambda w: pl.BlockSpec((tm, w), lambda i: (i, 0))
    const = lambda shape: pl.BlockSpec(shape, lambda i: (0,) * len(shape))
    attn = jax.ShapeDtypeStruct((T, DA_WIDTH), bf16)
    return pl.pallas_call(
        _inproj_body,
        grid=(T // tm,),
        in_specs=[row(D_MODEL), const((1, D_MODEL)), const((D_MODEL, CONV_PAD + GATE_COLS + 3 * DA_WIDTH)),
                  const((1, DA_WIDTH)), const((1, DA_WIDTH)), const((512, 512))],
        out_specs=[row(CONV_PAD), row(GATE_COLS), row(DA_WIDTH), row(DA_WIDTH), row(DA_WIDTH), row(DA_WIDTH)],
        out_shape=[jax.ShapeDtypeStruct((T, CONV_PAD), f32), jax.ShapeDtypeStruct((T, GATE_COLS), f32),
                   attn, attn, attn, attn],
        compiler_params=pltpu.CompilerParams(
            dimension_semantics=("arbitrary",), vmem_limit_bytes=VMEM_LIMIT),
        name="inproj",
    )(h, prm["norm_pre"], prm["w_in"], prm["q_norm"], prm["k_norm"], prm["seg"])


def _rwkv_prep_body(tiles_per_seq, main_ref, prev_ref, next_ref, cw_ref, cb_ref, w0_ref, wup_ref,
                    a0_ref, aup_ref, kk_ref, ka_ref, rk_ref, seg_ref, tri_ref,
                    slab_ref, decay_ref, v_ref, bonus_ref):
    tm = main_ref.shape[0]
    i = pl.program_id(0)
    local = i % tiles_per_seq
    is_first = local == 0
    is_last = local == tiles_per_seq - 1

    xm = main_ref[...]
    row = lax.broadcasted_iota(jnp.int32, (tm, 1), 0)
    prev_row = jnp.where(is_first, 0.0, prev_ref[7:8, :])
    next_row = jnp.where(is_last, 0.0, next_ref[0:1, :])
    xp = jnp.where(row == 0, prev_row, pltpu.roll(xm, 1, 0))
    xn = jnp.where(row == tm - 1, next_row, pltpu.roll(xm, tm - 1, 0))
    cw = cw_ref[...]
    c = cb_ref[...] + xp * cw[0:1] + xm * cw[1:2] + xn * cw[2:3]

    r = c[:, 0:512]
    k = c[:, 512:1024]
    v = c[:, 1024:1536]
    zw = c[:, 1536:1664]
    za = c[:, 1664:1792]

    wl = w0_ref[...] + _dot(jnp.tanh(zw).astype(bf16), wup_ref[...])
    logw = -DECAY_SCALE * _sigmoid(wl)
    a = _sigmoid(a0_ref[...] + _dot(za.astype(bf16), aup_ref[...]))
    seg = seg_ref[...]
    kk = k * kk_ref[...]
    kk = kk * lax.rsqrt(_segsum(kk * kk, seg) + KK_EPS)
    kmod = k * (1.0 + (a - 1.0) * ka_ref[...])
    kka = kk * a
    bonus_ref[...] = _segsum(r * kmod * rk_ref[...], seg) * v
    v_ref[...] = v.astype(bf16)

    for d in range(2):
        lw = logw[:, d * 512:(d + 1) * 512]
        cums = _dot_exact01(tri_ref[d], lw, 2)
        cum_in = cums[:tm]
        rev_ex = cums[tm:]
        e_pos = jnp.exp(cum_in)
        e_neg = jnp.exp(-cum_in)
        e_rev = jnp.exp(rev_ex)
        slab_ref[d, 0] = (r * e_pos).astype(bf16)
        slab_ref[d, 1] = (kmod * e_neg).astype(bf16)
        slab_ref[d, 2] = (-kk * jnp.exp(cum_in - lw)).astype(bf16)
        slab_ref[d, 3] = (kka * e_neg).astype(bf16)
        slab_ref[d, 4] = (kka * e_rev).astype(bf16)
        slab_ref[d, 5] = (kmod * e_rev).astype(bf16)
        total = jnp.exp(cum_in + rev_ex)
        for c in range(tm // CHUNK):
            decay_ref[d, c] = total[c * CHUNK:c * CHUNK + 1, :]


def _rwkv_prep(conv_in, seq_len, prm):
    T = conv_in.shape[0]
    tm = TOKEN_TILE
    tiles_per_seq = seq_len // tm
    n8 = T // 8
    const = lambda shape: pl.BlockSpec(shape, lambda i: (0,) * len(shape))
    return pl.pallas_call(
        functools.partial(_rwkv_prep_body, tiles_per_seq),
        grid=(T // tm,),
        in_specs=[
            pl.BlockSpec((tm, CONV_PAD), lambda i: (i, 0)),
            pl.BlockSpec((8, CONV_PAD), lambda i: (jnp.maximum(i * (tm // 8) - 1, 0), 0)),
            pl.BlockSpec((8, CONV_PAD), lambda i: (jnp.minimum((i + 1) * (tm // 8), n8 - 1), 0)),
            const((3, CONV_PAD)), const((1, CONV_PAD)), const((1, 1024)), const((128, 1024)),
            const((1, 512)), const((128, 512)), const((1, 512)), const((1, 512)), const((1, 512)),
            const((512, 512)), const((2, 2 * tm, tm)),
        ],
        out_specs=[
            pl.BlockSpec((2, 6, tm, 512), lambda i: (0, 0, i, 0)),
            pl.BlockSpec((2, tm // CHUNK, 1, 512), lambda i: (0, i, 0, 0)),
            pl.BlockSpec((tm, 512), lambda i: (i, 0)),
            pl.BlockSpec((tm, 512), lambda i: (i, 0)),
        ],
        out_shape=[
            jax.ShapeDtypeStruct((2, 6, T, 512), bf16),
            jax.ShapeDtypeStruct((2, T // CHUNK, 1, 512), f32),
            jax.ShapeDtypeStruct((T, 512), bf16),
            jax.ShapeDtypeStruct((T, 512), f32),
        ],
        compiler_params=pltpu.CompilerParams(
            dimension_semantics=("arbitrary",), vmem_limit_bytes=VMEM_LIMIT),
        name="rwkv_prep",
    )(conv_in, conv_in, conv_in, prm["conv_w"], prm["conv_b"], prm["w0"], prm["w_up"],
      prm["a0"], prm["a_up"], prm["k_k"], prm["k_a"], prm["r_k"], prm["seg"], prm["tri"])


def _mm(a, b):
    return _dot(a.astype(bf16), b.astype(bf16))


def _stack_heads(x, head_masks):
    x = x.astype(bf16)
    return jnp.concatenate([x * hm for hm in head_masks], axis=0)


def _rwkv_scan_body(chunks_per_step, slab_f_ref, slab_b_ref, decay_f_ref, decay_b_ref, v_f_ref, v_b_ref,
                    y_f_ref, y_b_ref, z_ref):
    L = CHUNK
    n_groups = RW_WIDTH // GROUP

    @pl.when(pl.program_id(1) == 0)
    def _():
        z_ref[...] = jnp.zeros_like(z_ref)

    t_row = lax.broadcasted_iota(jnp.int32, (L, GROUP), 0)
    t_col = lax.broadcasted_iota(jnp.int32, (L, GROUP), 1) % L
    eye_cat = jnp.where(t_row == t_col, 1.0, 0.0)
    strict = [(t_col < t_row).astype(f32), (t_col > t_row).astype(f32)]
    incl = [(t_col <= t_row).astype(f32), (t_col >= t_row).astype(f32)]
    rg = lax.broadcasted_iota(jnp.int32, (GROUP, GROUP), 0)
    cg = lax.broadcasted_iota(jnp.int32, (GROUP, GROUP), 1)
    blockdiag = ((rg // HEAD_DIM) == (cg // HEAD_DIM)).astype(f32)
    eye_g = rg == cg
    lane = lax.broadcasted_iota(jnp.int32, (1, GROUP), 1) // HEAD_DIM
    head_masks = [(lane == h).astype(bf16) for h in range(HEADS_PER_GROUP)]
    stack = lambda x: _stack_heads(x, head_masks)

    chains = []
    for d, (slab_ref, decay_ref, v_ref) in enumerate(
            ((slab_f_ref, decay_f_ref, v_f_ref), (slab_b_ref, decay_b_ref, v_b_ref))):
        for sub in (range(chunks_per_step) if d == 0 else reversed(range(chunks_per_step))):
            rows = slice(sub * L, (sub + 1) * L)
            for g in range(n_groups):
                lanes = slice(g * GROUP, (g + 1) * GROUP)
                chains.append(dict(
                    d=d, g=g, lanes=lanes, rows=rows,
                    r=slab_ref[0, rows, lanes], k=slab_ref[1, rows, lanes], a=slab_ref[2, rows, lanes],
                    b=slab_ref[3, rows, lanes], bp=slab_ref[4, rows, lanes], kp=slab_ref[5, rows, lanes],
                    pdiag=decay_ref[sub, :, lanes], v=v_ref[rows, lanes]))

    for ch in chains:
        d = ch["d"]
        ch["v_s"] = stack(ch["v"])
        gram = _dot_nt(jnp.concatenate([ch["a"], ch["r"]], axis=0),
                       jnp.concatenate([stack(ch["b"]), stack(ch["k"])], axis=0))
        ch["m_ab"] = gram[:L, :GROUP] * strict[d]
        ch["m_ak"] = gram[:L, GROUP:] * strict[d]
        ch["m_rb"] = gram[L:, :GROUP] * incl[d]
        ch["m_rk"] = gram[L:, GROUP:] * incl[d]
    for ch in chains:
        ch["t"] = eye_cat + ch["m_ab"]
        ch["q"] = _mm(ch["m_ab"], stack(ch["m_ab"]))
    for _ in range(4):
        for ch in chains:
            p = _mm(jnp.concatenate([ch["t"], ch["q"]], axis=0), stack(ch["q"]))
            ch["t"] = ch["t"] + p[:L]
            ch["q"] = p[L:]
    for ch in chains:
        ch["t"] = ch["t"] + _mm(ch["t"], stack(ch["q"]))
    for ch in chains:
        ch["w1"] = _mm(ch["m_ak"], ch["v_s"])
    for ch in chains:
        au = _mm(ch["t"], jnp.concatenate([stack(ch["a"]), stack(ch["w1"])], axis=1))
        ch["a_hat"] = au[:, :GROUP]
        ch["u_hat"] = au[:, GROUP:]
    for ch in chains:
        o1 = _mm(ch["m_rb"], jnp.concatenate([stack(ch["a_hat"]), stack(ch["u_hat"])], axis=1))
        ch["r_hat"] = ch["r"].astype(f32) + o1[:, :GROUP]
        ch["y_hat"] = o1[:, GROUP:] + _mm(ch["m_rk"], ch["v_s"])
    for ch in chains:
        ch["bk_t"] = jnp.concatenate([ch["bp"].astype(f32).T, ch["kp"].astype(f32).T], axis=1).astype(bf16)
        ch["ra"] = jnp.concatenate([ch["r_hat"], ch["a_hat"]], axis=0).astype(bf16)
        ch["decay_col"] = jnp.sum(jnp.where(eye_g, ch["pdiag"], 0.0), axis=1, keepdims=True)
    y_refs = (y_f_ref, y_b_ref)
    for ch in chains:
        z = z_ref[ch["d"], ch["g"]]
        rz = _dot(ch["ra"], z.astype(bf16))
        y_refs[ch["d"]][ch["rows"], ch["lanes"]] = rz[:L] + ch["y_hat"]
        u = (rz[L:] + ch["u_hat"]).astype(bf16)
        z_ref[ch["d"], ch["g"]] = (ch["decay_col"] * z
                                   + _dot(ch["bk_t"], jnp.concatenate([u, ch["v"]], axis=0)) * blockdiag)


def _rwkv_scan(slab, decay, v, batch, seq_len):
    T = v.shape[0]
    chunks_per_step = min(CHUNKS_PER_STEP, seq_len // CHUNK)
    rows = chunks_per_step * CHUNK
    steps = seq_len // rows
    fwd = lambda b, c: b * steps + c
    bwd = lambda b, c: b * steps + (steps - 1 - c)
    return pl.pallas_call(
        functools.partial(_rwkv_scan_body, chunks_per_step),
        grid=(batch, steps),
        in_specs=[
            pl.BlockSpec((None, 6, rows, 512), lambda b, c: (0, 0, fwd(b, c), 0)),
            pl.BlockSpec((None, 6, rows, 512), lambda b, c: (1, 0, bwd(b, c), 0)),
            pl.BlockSpec((None, chunks_per_step, 1, 512), lambda b, c: (0, fwd(b, c), 0, 0)),
            pl.BlockSpec((None, chunks_per_step, 1, 512), lambda b, c: (1, bwd(b, c), 0, 0)),
            pl.BlockSpec((rows, 512), lambda b, c: (fwd(b, c), 0)),
            pl.BlockSpec((rows, 512), lambda b, c: (bwd(b, c), 0)),
        ],
        out_specs=[
            pl.BlockSpec((rows, 512), lambda b, c: (fwd(b, c), 0)),
            pl.BlockSpec((rows, 512), lambda b, c: (bwd(b, c), 0)),
        ],
        out_shape=[jax.ShapeDtypeStruct((T, 512), f32), jax.ShapeDtypeStruct((T, 512), f32)],
        scratch_shapes=[pltpu.VMEM((2, RW_WIDTH // GROUP, GROUP, GROUP), f32)],
        compiler_params=pltpu.CompilerParams(
            dimension_semantics=("arbitrary", "arbitrary"), vmem_limit_bytes=VMEM_LIMIT),
        name="rwkv_scan",
    )(slab, slab, decay, decay, v, v)


def _flash_body(lam_init, tq, tk, seq_len, reach_ref, q0_ref, q1_ref, qaug_ref, k_ref, kaug_ref, v_ref,
                bias_ref, lam_ref, subln_ref, o_ref, s_ref, acc_ref):
    h = pl.program_id(1)
    qi = pl.program_id(2)
    nk = seq_len // tk
    lane_blocks = tk // DA_V_DIM

    kd = (qi * tq) // tk
    q_main = (q0_ref[...], q1_ref[...])
    q_aug = qaug_ref[...].astype(f32)
    ones_col = jnp.where(lax.broadcasted_iota(jnp.int32, (2 * tk, DA_V_DIM), 1) == 0, 1.0, 0.0).astype(bf16)

    def tile_start(kt):
        return kt * tk if isinstance(kt, int) else pl.multiple_of(kt * tk, tk)

    def scores(kt, running):
        start = tile_start(kt)
        k = jnp.concatenate([k_ref[pl.ds(start, tk), :], kaug_ref[pl.ds(start, tk), :]], axis=1)
        sign = jnp.where(kt < kd, 1.0, jnp.where(kt > kd, -1.0, 0.0)).astype(f32)
        aug = (q_aug * sign).astype(bf16)
        explicit_bias = jnp.where(kt == kd, 1.0, 0.0).astype(f32) * bias_ref[...]
        out = []
        for mi in range(2):
            s = _dot_nt(jnp.concatenate([q_main[mi], aug], axis=1), k) - explicit_bias
            s_ref[mi, kt] = s
            mx = running[mi]
            for c in range(lane_blocks):
                mx = jnp.maximum(mx, s[:, c * DA_V_DIM:(c + 1) * DA_V_DIM])
            out.append(mx)
        return tuple(out)

    span = 2 * tk
    reach = reach_ref[h]
    first_pair = jnp.maximum(qi * tq - reach, 0) // span
    n_pairs = jnp.minimum((qi * tq + tq - 1 + reach) // span + 1, nk // 2) - first_pair

    def score_pair(j, running):
        return scores(2 * j + 1, scores(2 * j, running))

    running = (jnp.full((tq, DA_V_DIM), -jnp.inf, f32),) * 2
    running = lax.fori_loop(
        0, n_pairs // 2,
        lambda i, r: score_pair(first_pair + 2 * i + 1, score_pair(first_pair + 2 * i, r)), running)
    running = lax.fori_loop(
        0, n_pairs % 2, lambda i, r: score_pair(first_pair + n_pairs - 1, r), running)
    if nk % 2:
        running = scores(nk - 1, running)
    row_max = [jnp.broadcast_to(jnp.max(mx, axis=-1, keepdims=True), (tq, DA_V_DIM)) for mx in running]

    acc_ref[...] = jnp.zeros_like(acc_ref)

    def weighted_values(mi, first, count):
        v = jnp.concatenate([v_ref[pl.ds(tile_start(first), count * tk), :], ones_col[:count * tk]], axis=1)
        ps = [jnp.exp(s_ref[mi, first + j, :, c * DA_V_DIM:(c + 1) * DA_V_DIM] - row_max[mi]).astype(bf16)
              for j in range(count) for c in range(lane_blocks)]
        return _dot(jnp.concatenate(ps, axis=1), v)

    def two_pairs(i, carry):
        j = first_pair + 2 * i
        for mi in range(2):
            acc_ref[mi] += weighted_values(mi, 2 * j, 2) + weighted_values(mi, 2 * j + 2, 2)
        return carry

    def last_pair(i, carry):
        for mi in range(2):
            acc_ref[mi] += weighted_values(mi, 2 * (first_pair + n_pairs - 1), 2)
        return carry

    lax.fori_loop(0, n_pairs // 2, two_pairs, 0)
    lax.fori_loop(0, n_pairs % 2, last_pair, 0)
    if nk % 2:
        for mi in range(2):
            acc_ref[mi] += weighted_values(mi, nk - 1, 1)

    lv = lam_ref[...]
    lam = (jnp.exp(jnp.sum(lv[0:1] * lv[1:2], axis=-1, keepdims=True))
           - jnp.exp(jnp.sum(lv[2:3] * lv[3:4], axis=-1, keepdims=True)) + lam_init)
    acc0 = acc_ref[0]
    acc1 = acc_ref[1]
    o = (acc0[:, :DA_V_DIM] / acc0[:, DA_V_DIM:DA_V_DIM + 1]
         - lam * (acc1[:, :DA_V_DIM] / acc1[:, DA_V_DIM:DA_V_DIM + 1]))
    ms = jnp.mean(o * o, axis=-1, keepdims=True)
    o_ref[...] = o * lax.rsqrt(ms + NORM_EPS) * subln_ref[...] * (1.0 - lam_init)


def _alibi_reach(q_norm, k_norm):
    gq = jnp.max(jnp.abs(q_norm))
    gk = jnp.max(jnp.abs(k_norm))
    slopes = 2.0 ** (-(ALIBI_MAX_BIAS / DA_HEADS) * jnp.arange(1, DA_HEADS + 1, dtype=f32))
    dist = (1.05 * 16.0 * gq * gk + EXP2_UNDERFLOW * math.log(2.0)) / slopes
    return jnp.minimum(jnp.ceil(dist), float(MAX_REACH)).astype(jnp.int32)


def _alibi_columns(seq_len):
    pos = jnp.arange(seq_len, dtype=jnp.int32)
    hi = (pos // POS_SPLIT).astype(f32)[None, :, None]
    lo = (pos % POS_SPLIT).astype(f32)[None, :, None]
    slopes = (2.0 ** (-(ALIBI_MAX_BIAS / DA_HEADS) * jnp.arange(1, DA_HEADS + 1, dtype=f32)))[:, None, None]
    col = jnp.arange(DA_V_DIM, dtype=jnp.int32)[None, None, :]
    q_cols = jnp.where(col == 0, -slopes * POS_SPLIT * hi,
                       jnp.where(col == 1, -slopes * lo, jnp.where(col < 4, 1.0, 0.0)))
    k_cols = jnp.where(col < 2, 1.0,
                       jnp.where(col == 2, slopes * POS_SPLIT * hi, jnp.where(col == 3, slopes * lo, 0.0)))
    return q_cols.astype(bf16), k_cols.astype(bf16)


def _flash(q0, q1, k, v, reach, lam_vec, subln, lam_init, batch, seq_len):
    T = q0.shape[0]
    tq = min(FLASH_TQ, seq_len)
    tk = min(FLASH_TK, seq_len)
    nq = seq_len // tq
    q_cols, k_cols = _alibi_columns(seq_len)
    offsets = tk // tq
    qk_dist = jnp.abs(jnp.arange(tq, dtype=jnp.int32)[None, :, None] - jnp.arange(tk, dtype=jnp.int32)[None, None, :]
                      + tq * jnp.arange(offsets, dtype=jnp.int32)[:, None, None]).astype(f32)
    slopes = 2.0 ** (-(ALIBI_MAX_BIAS / DA_HEADS) * jnp.arange(1, DA_HEADS + 1, dtype=f32))
    overlap_bias = slopes[:, None, None, None] * qk_dist[None]
    qspec = pl.BlockSpec((tq, DA_V_DIM), lambda b, h, i, reach_ref: (b * nq + i, h))
    kspec = pl.BlockSpec((seq_len, DA_V_DIM), lambda b, h, i, reach_ref: (b, h), pipeline_mode=pl.Buffered(1))
    return pl.pallas_call(
        functools.partial(_flash_body, lam_init, tq, tk, seq_len),
        grid_spec=pltpu.PrefetchScalarGridSpec(
            num_scalar_prefetch=1,
            grid=(batch, DA_HEADS, nq),
            in_specs=[qspec, qspec,
                      pl.BlockSpec((None, tq, DA_V_DIM), lambda b, h, i, reach_ref: (h, i, 0)),
                      kspec,
                      pl.BlockSpec((None, seq_len, DA_V_DIM), lambda b, h, i, reach_ref: (h, 0, 0),
                                   pipeline_mode=pl.Buffered(1)),
                      kspec,
                      pl.BlockSpec((None, None, tq, tk), lambda b, h, i, reach_ref: (h, i % offsets, 0, 0)),
                      pl.BlockSpec((4, HEAD_DIM), lambda b, h, i, reach_ref: (0, 0)),
                      pl.BlockSpec((1, DA_V_DIM), lambda b, h, i, reach_ref: (0, 0))],
            out_specs=pl.BlockSpec((tq, DA_V_DIM), lambda b, h, i, reach_ref: (b * nq + i, h)),
            scratch_shapes=[pltpu.VMEM((2, seq_len // tk, tq, tk), f32), pltpu.VMEM((2, tq, DA_EXT), f32)],
        ),
        out_shape=jax.ShapeDtypeStruct((T, DA_WIDTH), f32),
        compiler_params=pltpu.CompilerParams(
            dimension_semantics=("arbitrary", "arbitrary", "arbitrary"),
            vmem_limit_bytes=FLASH_VMEM_LIMIT),
        name="flash_diff_attn",
    )(reach, q0, q1, q_cols, k, k_cols, v, overlap_bias, lam_vec, subln)


def _out_body(h_ref, yf_ref, yb_ref, bonus_ref, grw_ref, gda_ref, o_ref, p_ref, seg_ref, lng_ref, lnb_ref,
              wout_ref, pproj_ref, pnorm_ref, gw_ref, gb_ref, out_ref):
    seg = seg_ref[...]
    inv_n = 1.0 / HEAD_DIM
    y = yf_ref[...] + yb_ref[...]
    mu = _segsum(y, seg) * inv_n
    yc = y - mu
    var = _segsum(yc * yc, seg) * inv_n
    y_rw = (yc * lax.rsqrt(var + LN_X_EPS) * lng_ref[...] + lnb_ref[...] + bonus_ref[...])
    y_rw = y_rw * _silu(grw_ref[...])
    y_da = o_ref[...] * _silu(gda_ref[...])
    h1 = (h_ref[...] + _dot(y_rw.astype(bf16), wout_ref[0:512, :])
          + _dot(y_da.astype(bf16), wout_ref[512:1024, :]))
    e = _dot(p_ref[...].astype(bf16), pproj_ref[...])
    e = e * lax.rsqrt(jnp.mean(e * e, axis=-1, keepdims=True) + NORM_EPS) * pnorm_ref[...]
    gate = _sigmoid(_dot(h1.astype(bf16), gw_ref[...]) + gb_ref[...])
    out_ref[...] = h1 + gate * e


def _out_stage(h, y_f, y_b, bonus, gates, o, p, prm):
    T = h.shape[0]
    tm = PROJ_TILE
    row = lambda w: pl.BlockSpec((tm, w), lambda i: (i, 0))
    const = lambda shape: pl.BlockSpec(shape, lambda i: (0,) * len(shape))
    return pl.pallas_call(
        _out_body,
        grid=(T // tm,),
        in_specs=[row(D_MODEL), row(512), row(512), row(512),
                  pl.BlockSpec((tm, 512), lambda i: (i, 0)),
                  pl.BlockSpec((tm, 512), lambda i: (i, 1)),
                  row(512), row(D_PLE),
                  const((512, 512)), const((1, 512)), const((1, 512)),
                  const((D_MODEL, D_MODEL)), const((D_PLE, D_MODEL)), const((1, D_MODEL)),
                  const((D_MODEL, D_MODEL)), const((1, D_MODEL))],
        out_specs=row(D_MODEL),
        out_shape=jax.ShapeDtypeStruct((T, D_MODEL), f32),
        compiler_params=pltpu.CompilerParams(
            dimension_semantics=("arbitrary",), vmem_limit_bytes=VMEM_LIMIT),
        name="out_stage",
    )(h, y_f, y_b, bonus, gates, gates, o, p, prm["seg"], prm["ln_g"], prm["ln_b"], prm["w_out"],
      prm["ple_proj"], prm["ple_norm"], prm["gate_w"], prm["gate_b"])


def _chunk_triangles(tm):
    t = np.arange(tm)
    same = (t[:, None] // CHUNK) == (t[None, :] // CHUNK)
    low_incl = same & (t[None, :] <= t[:, None])
    up_strict = same & (t[None, :] > t[:, None])
    up_incl = same & (t[None, :] >= t[:, None])
    low_strict = same & (t[None, :] < t[:, None])
    tri = np.stack([np.concatenate([low_incl, up_strict], 0), np.concatenate([up_incl, low_strict], 0)])
    return jnp.asarray(tri.astype(np.float32), dtype=bf16)


def _layer_params(i, norm_pre, w_in, w_out, rw_conv_w, rw_conv_b, rw_w0, rw_w_up, rw_a0, rw_a_up, rw_k_k,
                  rw_k_a, rw_r_k, rw_ln_g, rw_ln_b, da_q_norm, da_k_norm, da_lambda, da_subln, ple_proj,
                  ple_norm, ple_gate_w, ple_gate_b):
    pad = CONV_PAD - (CONV_COLS)
    w = w_in[i]
    o1 = CONV_COLS + RW_WIDTH
    o4 = o1 + 3 * DA_WIDTH
    w_pad = jnp.concatenate(
        [w[:, :CONV_COLS], jnp.zeros((D_MODEL, pad), f32), w[:, CONV_COLS:o1], w[:, o4:], w[:, o1:o4]],
        axis=1).astype(bf16)
    wup = rw_w_up[i]
    zeros = jnp.zeros((W_LORA, RW_WIDTH), f32)
    w_up_bd = jnp.concatenate(
        [jnp.concatenate([wup[0], zeros], 1), jnp.concatenate([zeros, wup[1]], 1)], 0).astype(bf16)
    a_up_pad = jnp.concatenate([rw_a_up[i], jnp.zeros((64, RW_WIDTH), f32)], 0).astype(bf16)
    lane = np.arange(512)
    seg = jnp.asarray((lane[:, None] // HEAD_DIM == lane[None, :] // HEAD_DIM).astype(np.float32), dtype=bf16)
    return dict(
        norm_pre=norm_pre[i].reshape(1, D_MODEL),
        w_in=w_pad,
        conv_w=jnp.pad(rw_conv_w[i], ((0, 0), (0, pad))),
        conv_b=jnp.pad(rw_conv_b[i], (0, pad)).reshape(1, CONV_PAD),
        w0=rw_w0[i].reshape(1, 2 * RW_WIDTH),
        w_up=w_up_bd,
        a0=rw_a0[i].reshape(1, RW_WIDTH),
        a_up=a_up_pad,
        k_k=rw_k_k[i].reshape(1, RW_WIDTH),
        k_a=rw_k_a[i].reshape(1, RW_WIDTH),
        r_k=rw_r_k[i].reshape(1, RW_WIDTH),
        ln_g=rw_ln_g[i].reshape(1, RW_WIDTH),
        ln_b=rw_ln_b[i].reshape(1, RW_WIDTH),
        q_norm=jnp.tile(da_q_norm[i].reshape(1, 2 * HEAD_DIM), (1, DA_HEADS)),
        k_norm=jnp.tile(da_k_norm[i].reshape(1, 2 * HEAD_DIM), (1, DA_HEADS)),
        reach=_alibi_reach(da_q_norm[i], da_k_norm[i]),
        lam_vec=da_lambda[i],
        subln=da_subln[i].reshape(1, DA_V_DIM),
        w_out=w_out[i].astype(bf16),
        ple_proj=ple_proj[i].astype(bf16),
        ple_norm=ple_norm[i].reshape(1, D_MODEL),
        gate_w=ple_gate_w[i].astype(bf16),
        gate_b=ple_gate_b[i].reshape(1, D_MODEL),
        seg=seg,
        tri=_chunk_triangles(TOKEN_TILE),
    )


def _layer(h, p, lam_init, prm, batch, seq_len):
    conv_in, gates, q0, q1, k, v = _inproj(h, prm)
    slab, decay, v_rw, bonus = _rwkv_prep(conv_in, seq_len, prm)
    y_f, y_b = _rwkv_scan(slab, decay, v_rw, batch, seq_len)
    o = _flash(q0, q1, k, v, prm["reach"], prm["lam_vec"], prm["subln"], lam_init, batch, seq_len)
    return _out_stage(h, y_f, y_b, bonus, gates, o, p, prm)


def _trunk(x, p, layers):
    batch, seq_len, _ = x.shape
    h = x.reshape(batch * seq_len, D_MODEL)
    for i, prm in enumerate(layers):
        lam_init = 0.8 - 0.6 * math.exp(-0.3 * i)
        h = _layer(h, p[i].reshape(batch * seq_len, D_PLE), lam_init, prm, batch, seq_len)
    return h.reshape(batch, seq_len, D_MODEL)


def kernel(x_prompt, x_sample, p_prompt, p_sample, norm_pre, w_in, w_out, rw_conv_w, rw_conv_b, rw_w0, rw_w_up, rw_a0, rw_a_up, rw_k_k, rw_k_a, rw_r_k, rw_ln_g, rw_ln_b, da_q_norm, da_k_norm, da_lambda, da_subln, ple_proj, ple_norm, ple_gate_w, ple_gate_b):
    depth = norm_pre.shape[0]
    layers = [_layer_params(i, norm_pre, w_in, w_out, rw_conv_w, rw_conv_b, rw_w0, rw_w_up, rw_a0, rw_a_up,
                            rw_k_k, rw_k_a, rw_r_k, rw_ln_g, rw_ln_b, da_q_norm, da_k_norm, da_lambda,
                            da_subln, ple_proj, ple_norm, ple_gate_w, ple_gate_b) for i in range(depth)]
    return (_trunk(x_prompt, p_prompt, layers), _trunk(x_sample, p_sample, layers))
```

```python
import functools
import math

import numpy as np
import jax
import jax.numpy as jnp
from jax import lax
from jax.experimental import pallas as pl
from jax.experimental.pallas import tpu as pltpu

f32 = jnp.float32
bf16 = jnp.bfloat16

D_MODEL = 1024
D_PLE = 256
RW_WIDTH = 512
HEAD_DIM = 64
W_LORA = 64
CONV_COLS = 3 * RW_WIDTH + 2 * W_LORA + 64
CONV_PAD = 1792
DA_HEADS = 4
DA_V_DIM = 128
DA_WIDTH = 512
DA_EXT = 256
GATE_COLS = 2 * 512
NORM_EPS = 1e-6
LN_X_EPS = 64e-5
KK_EPS = 1e-12
ALIBI_MAX_BIAS = 8.0
DECAY_SCALE = math.exp(-0.5)
EXP2_UNDERFLOW = 160.0
MAX_REACH = 1 << 24
POS_SPLIT = 128

CHUNK = 64
CHUNKS_PER_STEP = 4
GROUP = 256
HEADS_PER_GROUP = GROUP // HEAD_DIM
TOKEN_TILE = 256
PROJ_TILE = 512
FLASH_TQ = 512
FLASH_TK = 512
VMEM_LIMIT = 48 * 1024 * 1024
FLASH_VMEM_LIMIT = 56 * 1024 * 1024


def _dot(a, b):
    return jnp.dot(a, b, preferred_element_type=f32)


def _dot_nt(a, b):
    return lax.dot_general(a, b, (((1,), (1,)), ((), ())), preferred_element_type=f32)


def _split2(x):
    hi = x.astype(bf16)
    lo = (x - hi.astype(f32)).astype(bf16)
    return hi, lo


def _split3(x):
    hi = x.astype(bf16)
    r1 = x - hi.astype(f32)
    mid = r1.astype(bf16)
    lo = (r1 - mid.astype(f32)).astype(bf16)
    return hi, mid, lo


def _dot_exact01(m01, x, parts):
    ps = _split3(x) if parts == 3 else _split2(x)
    acc = _dot(m01, ps[0])
    for p in ps[1:]:
        acc = acc + _dot(m01, p)
    return acc


def _segsum(x, seg):
    hi, lo = _split2(x)
    return _dot(hi, seg) + _dot(lo, seg)


def _sigmoid(x):
    return 1.0 / (1.0 + jnp.exp(-x))


def _silu(x):
    return x * _sigmoid(x)


def _rwkv_chunk_operands(xm, prev_row, next_row, cw_ref, cb_ref, w0_ref, wup_ref, a0_ref, aup_ref, kk_ref,
                         ka_ref, rk_ref, seg, tri_ref, slab_ref, decay_ref, v_ref, bonus_ref):
    tm = xm.shape[0]
    row = lax.broadcasted_iota(jnp.int32, (tm, 1), 0)
    xp = jnp.where(row == 0, prev_row, pltpu.roll(xm, 1, 0))
    xn = jnp.where(row == tm - 1, next_row, pltpu.roll(xm, tm - 1, 0))
    cw = cw_ref[...]
    c = cb_ref[...] + xp * cw[0:1] + xm * cw[1:2] + xn * cw[2:3]

    r = c[:, 0:512]
    k = c[:, 512:1024]
    v = c[:, 1024:1536]
    zw = c[:, 1536:1664]
    za = c[:, 1664:1792]

    wl = w0_ref[...] + _dot(jnp.tanh(zw).astype(bf16), wup_ref[...])
    logw = -DECAY_SCALE * _sigmoid(wl)
    a = _sigmoid(a0_ref[...] + _dot(za.astype(bf16), aup_ref[...]))
    kk = k * kk_ref[...]
    kk = kk * lax.rsqrt(_segsum(kk * kk, seg) + KK_EPS)
    kmod = k * (1.0 + (a - 1.0) * ka_ref[...])
    kka = kk * a
    bonus_ref[...] = _segsum(r * kmod * rk_ref[...], seg) * v
    v_ref[...] = v.astype(bf16)

    for d in range(2):
        lw = logw[:, d * 512:(d + 1) * 512]
        cums = _dot_exact01(tri_ref[d], lw, 2)
        cum_in = cums[:tm]
        rev_ex = cums[tm:]
        e_pos = jnp.exp(cum_in)
        e_neg = jnp.exp(-cum_in)
        e_rev = jnp.exp(rev_ex)
        slab_ref[d, 0] = (r * e_pos).astype(bf16)
        slab_ref[d, 1] = (kmod * e_neg).astype(bf16)
        slab_ref[d, 2] = (-kk * jnp.exp(cum_in - lw)).astype(bf16)
        slab_ref[d, 3] = (kka * e_neg).astype(bf16)
        slab_ref[d, 4] = (kka * e_rev).astype(bf16)
        slab_ref[d, 5] = (kmod * e_rev).astype(bf16)
        total = jnp.exp(cum_in + rev_ex)
        for c in range(tm // CHUNK):
            decay_ref[d, c] = total[c * CHUNK:c * CHUNK + 1, :]


def _front_body(tiles_per_seq, n_tiles, x_ref, xnext_ref, g_ref, w_ref, qn_ref, kn_ref, seg_ref,
                cw_ref, cb_ref, w0_ref, wup_ref, a0_ref, aup_ref, kk_ref, ka_ref, rk_ref, tri_ref,
                gates_ref, q0_ref, q1_ref, k_ref, v_ref, slab_ref, decay_ref, vrw_ref, bonus_ref,
                conv_ref, last_row_ref):
    tm = x_ref.shape[0]
    i = pl.program_id(0)
    local = i % tiles_per_seq
    is_first = local == 0
    is_last = local == tiles_per_seq - 1
    g = g_ref[...]
    seg = seg_ref[...]

    def normed(x):
        ms = jnp.mean(x * x, axis=-1, keepdims=True)
        return (x * lax.rsqrt(ms + NORM_EPS) * g).astype(bf16)

    def project_conv(x, slot):
        un = normed(x)
        for c0 in range(0, CONV_PAD, 256):
            conv_ref[slot, :, c0:c0 + 256] = _dot(un, w_ref[:, c0:c0 + 256])

    @pl.when(i == 0)
    def _():
        conv_ref[1] = jnp.zeros((tm, CONV_PAD), f32)
        last_row_ref[...] = jnp.zeros_like(last_row_ref)
        project_conv(x_ref[...], 0)

    project_conv(xnext_ref[...], (i + 1) % 2)

    u = normed(x_ref[...])
    for c0 in range(0, GATE_COLS, 256):
        gates_ref[:, c0:c0 + 256] = _dot(u, w_ref[:, CONV_PAD + c0:CONV_PAD + c0 + 256])
    base = CONV_PAD + GATE_COLS
    q = _dot(u, w_ref[:, base:base + DA_WIDTH])
    k = _dot(u, w_ref[:, base + DA_WIDTH:base + 2 * DA_WIDTH])
    v_ref[...] = _dot(u, w_ref[:, base + 2 * DA_WIDTH:base + 3 * DA_WIDTH]).astype(bf16)
    inv_d = 1.0 / HEAD_DIM
    qn = q * lax.rsqrt(_segsum(q * q, seg) * inv_d + NORM_EPS) * qn_ref[...] * (HEAD_DIM ** -0.5)
    kn = k * lax.rsqrt(_segsum(k * k, seg) * inv_d + NORM_EPS) * kn_ref[...]
    lane = lax.broadcasted_iota(jnp.int32, (1, DA_WIDTH), 1)
    first_map = (lane // HEAD_DIM) % 2 == 0
    q0_ref[...] = jnp.where(first_map, qn, 0.0).astype(bf16)
    q1_ref[...] = jnp.where(first_map, 0.0, qn).astype(bf16)
    k_ref[...] = kn.astype(bf16)

    xm = conv_ref[i % 2]
    prev_row = jnp.where(is_first, 0.0, last_row_ref[...])
    next_row = jnp.where(is_last, 0.0, conv_ref[(i + 1) % 2, 0:1, :])
    last_row_ref[...] = xm[tm - 1:tm, :]
    _rwkv_chunk_operands(xm, prev_row, next_row, cw_ref, cb_ref, w0_ref, wup_ref, a0_ref, aup_ref, kk_ref,
                         ka_ref, rk_ref, seg, tri_ref, slab_ref, decay_ref, vrw_ref, bonus_ref)


def _front(h, seq_len, prm):
    T = h.shape[0]
    tm = TOKEN_TILE
    n_tiles = T // tm
    row = lambda w: pl.BlockSpec((tm, w), lambda i: (i, 0))
    const = lambda shape: pl.BlockSpec(shape, lambda i: (0,) * len(shape))
    attn = jax.ShapeDtypeStruct((T, DA_WIDTH), bf16)
    return pl.pallas_call(
        functools.partial(_front_body, seq_len // tm, n_tiles),
        grid=(n_tiles,),
        in_specs=[
            row(D_MODEL),
            pl.BlockSpec((tm, D_MODEL), lambda i: (jnp.minimum(i + 1, n_tiles - 1), 0)),
            const((1, D_MODEL)), const((D_MODEL, CONV_PAD + GATE_COLS + 3 * DA_WIDTH)),
            const((1, DA_WIDTH)), const((1, DA_WIDTH)), const((512, 512)),
            const((3, CONV_PAD)), const((1, CONV_PAD)), const((1, 1024)), const((128, 1024)),
            const((1, 512)), const((128, 512)), const((1, 512)), const((1, 512)), const((1, 512)),
            const((2, 2 * tm, tm)),
        ],
        out_specs=[
            row(GATE_COLS), row(DA_WIDTH), row(DA_WIDTH), row(DA_WIDTH), row(DA_WIDTH),
            pl.BlockSpec((2, 6, tm, 512), lambda i: (0, 0, i, 0)),
            pl.BlockSpec((2, tm // CHUNK, 1, 512), lambda i: (0, i, 0, 0)),
            row(512), row(512),
        ],
        out_shape=[
            jax.ShapeDtypeStruct((T, GATE_COLS), f32), attn, attn, attn, attn,
            jax.ShapeDtypeStruct((2, 6, T, 512), bf16),
            jax.ShapeDtypeStruct((2, T // CHUNK, 1, 512), f32),
            jax.ShapeDtypeStruct((T, 512), bf16),
            jax.ShapeDtypeStruct((T, 512), f32),
        ],
        scratch_shapes=[pltpu.VMEM((2, tm, CONV_PAD), f32), pltpu.VMEM((1, CONV_PAD), f32)],
        compiler_params=pltpu.CompilerParams(
            dimension_semantics=("arbitrary",), vmem_limit_bytes=VMEM_LIMIT),
        name="front",
    )(h, h, prm["norm_pre"], prm["w_in"], prm["q_norm"], prm["k_norm"], prm["seg"],
      prm["conv_w"], prm["conv_b"], prm["w0"], prm["w_up"], prm["a0"], prm["a_up"], prm["k_k"], prm["k_a"],
      prm["r_k"], prm["tri"])


def _mm(a, b):
    return _dot(a.astype(bf16), b.astype(bf16))


def _stack_heads(x, head_masks):
    x = x.astype(bf16)
    return jnp.concatenate([x * hm for hm in head_masks], axis=0)


def _rwkv_scan_body(chunks_per_step, slab_f_ref, slab_b_ref, decay_f_ref, decay_b_ref, v_f_ref, v_b_ref,
                    y_f_ref, y_b_ref, z_ref):
    L = CHUNK
    n_groups = RW_WIDTH // GROUP

    @pl.when(pl.program_id(1) == 0)
    def _():
        z_ref[...] = jnp.zeros_like(z_ref)

    t_row = lax.broadcasted_iota(jnp.int32, (L, GROUP), 0)
    t_col = lax.broadcasted_iota(jnp.int32, (L, GROUP), 1) % L
    eye_cat = jnp.where(t_row == t_col, 1.0, 0.0)
    strict = [(t_col < t_row).astype(f32), (t_col > t_row).astype(f32)]
    incl = [(t_col <= t_row).astype(f32), (t_col >= t_row).astype(f32)]
    rg = lax.broadcasted_iota(jnp.int32, (GROUP, GROUP), 0)
    cg = lax.broadcasted_iota(jnp.int32, (GROUP, GROUP), 1)
    blockdiag = ((rg // HEAD_DIM) == (cg // HEAD_DIM)).astype(f32)
    eye_g = rg == cg
    lane = lax.broadcasted_iota(jnp.int32, (1, GROUP), 1) // HEAD_DIM
    head_masks = [(lane == h).astype(bf16) for h in range(HEADS_PER_GROUP)]
    stack = lambda x: _stack_heads(x, head_masks)

    chains = []
    for d, (slab_ref, decay_ref, v_ref) in enumerate(
            ((slab_f_ref, decay_f_ref, v_f_ref), (slab_b_ref, decay_b_ref, v_b_ref))):
        for sub in (range(chunks_per_step) if d == 0 else reversed(range(chunks_per_step))):
            rows = slice(sub * L, (sub + 1) * L)
            for g in range(n_groups):
                lanes = slice(g * GROUP, (g + 1) * GROUP)
                chains.append(dict(
                    d=d, g=g, lanes=lanes, rows=rows,
                    r=slab_ref[0, rows, lanes], k=slab_ref[1, rows, lanes], a=slab_ref[2, rows, lanes],
                    b=slab_ref[3, rows, lanes], bp=slab_ref[4, rows, lanes], kp=slab_ref[5, rows, lanes],
                    pdiag=decay_ref[sub, :, lanes], v=v_ref[rows, lanes]))

    for ch in chains:
        d = ch["d"]
        ch["v_s"] = stack(ch["v"])
        gram = _dot_nt(jnp.concatenate([ch["a"], ch["r"]], axis=0),
                       jnp.concatenate([stack(ch["b"]), stack(ch["k"])], axis=0))
        ch["m_ab"] = gram[:L, :GROUP] * strict[d]
        ch["m_ak"] = gram[:L, GROUP:] * strict[d]
        ch["m_rb"] = gram[L:, :GROUP] * incl[d]
        ch["m_rk"] = gram[L:, GROUP:] * incl[d]
    for ch in chains:
        ch["t"] = eye_cat + ch["m_ab"]
        ch["q"] = _mm(ch["m_ab"], stack(ch["m_ab"]))
    for _ in range(4):
        for ch in chains:
            p = _mm(jnp.concatenate([ch["t"], ch["q"]], axis=0), stack(ch["q"]))
            ch["t"] = ch["t"] + p[:L]
            ch["q"] = p[L:]
    for ch in chains:
        ch["t"] = ch["t"] + _mm(ch["t"], stack(ch["q"]))
    for ch in chains:
        ch["w1"] = _mm(ch["m_ak"], ch["v_s"])
    for ch in chains:
        au = _mm(ch["t"], jnp.concatenate([stack(ch["a"]), stack(ch["w1"])], axis=1))
        ch["a_hat"] = au[:, :GROUP]
        ch["u_hat"] = au[:, GROUP:]
    for ch in chains:
        o1 = _mm(ch["m_rb"], jnp.concatenate([stack(ch["a_hat"]), stack(ch["u_hat"])], axis=1))
        ch["r_hat"] = ch["r"].astype(f32) + o1[:, :GROUP]
        ch["y_hat"] = o1[:, GROUP:] + _mm(ch["m_rk"], ch["v_s"])
    for ch in chains:
        ch["bk_t"] = jnp.concatenate([ch["bp"].astype(f32).T, ch["kp"].astype(f32).T], axis=1).astype(bf16)
        ch["ra"] = jnp.concatenate([ch["r_hat"], ch["a_hat"]], axis=0).astype(bf16)
        ch["decay_col"] = jnp.sum(jnp.where(eye_g, ch["pdiag"], 0.0), axis=1, keepdims=True)
    y_refs = (y_f_ref, y_b_ref)
    for ch in chains:
        z = z_ref[ch["d"], ch["g"]]
        rz = _dot(ch["ra"], z.astype(bf16))
        y_refs[ch["d"]][ch["rows"], ch["lanes"]] = rz[:L] + ch["y_hat"]
        u = (rz[L:] + ch["u_hat"]).astype(bf16)
        z_ref[ch["d"], ch["g"]] = (ch["decay_col"] * z
                                   + _dot(ch["bk_t"], jnp.concatenate([u, ch["v"]], axis=0)) * blockdiag)


def _rwkv_scan(slab, decay, v, batch, seq_len):
    T = v.shape[0]
    chunks_per_step = min(CHUNKS_PER_STEP, seq_len // CHUNK)
    rows = chunks_per_step * CHUNK
    steps = seq_len // rows
    fwd = lambda b, c: b * steps + c
    bwd = lambda b, c: b * steps + (steps - 1 - c)
    return pl.pallas_call(
        functools.partial(_rwkv_scan_body, chunks_per_step),
        grid=(batch, steps),
        in_specs=[
            pl.BlockSpec((None, 6, rows, 512), lambda b, c: (0, 0, fwd(b, c), 0)),
            pl.BlockSpec((None, 6, rows, 512), lambda b, c: (1, 0, bwd(b, c), 0)),
            pl.BlockSpec((None, chunks_per_step, 1, 512), lambda b, c: (0, fwd(b, c), 0, 0)),
            pl.BlockSpec((None, chunks_per_step, 1, 512), lambda b, c: (1, bwd(b, c), 0, 0)),
            pl.BlockSpec((rows, 512), lambda b, c: (fwd(b, c), 0)),
            pl.BlockSpec((rows, 512), lambda b, c: (bwd(b, c), 0)),
        ],
        out_specs=[
            pl.BlockSpec((rows, 512), lambda b, c: (fwd(b, c), 0)),
            pl.BlockSpec((rows, 512), lambda b, c: (bwd(b, c), 0)),
        ],
        out_shape=[jax.ShapeDtypeStruct((T, 512), f32), jax.ShapeDtypeStruct((T, 512), f32)],
        scratch_shapes=[pltpu.VMEM((2, RW_WIDTH // GROUP, GROUP, GROUP), f32)],
        compiler_params=pltpu.CompilerParams(
            dimension_semantics=("arbitrary", "arbitrary"), vmem_limit_bytes=VMEM_LIMIT),
        name="rwkv_scan",
    )(slab, slab, decay, decay, v, v)


def _flash_body(lam_init, tq, tk, seq_len, reach_ref, q0_ref, q1_ref, qaug_ref, k_ref, kaug_ref, v_ref,
                bias_ref, lam_ref, subln_ref, o_ref, s_ref, acc_ref):
    h = pl.program_id(1)
    qi = pl.program_id(2)
    nk = seq_len // tk
    lane_blocks = tk // DA_V_DIM

    kd = (qi * tq) // tk
    q_main = (q0_ref[...], q1_ref[...])
    q_aug = qaug_ref[...].astype(f32)
    ones_col = jnp.where(lax.broadcasted_iota(jnp.int32, (2 * tk, DA_V_DIM), 1) == 0, 1.0, 0.0).astype(bf16)

    def tile_start(kt):
        return kt * tk if isinstance(kt, int) else pl.multiple_of(kt * tk, tk)

    def scores(kt, running):
        start = tile_start(kt)
        k = jnp.concatenate([k_ref[pl.ds(start, tk), :], kaug_ref[pl.ds(start, tk), :]], axis=1)
        sign = jnp.where(kt < kd, 1.0, jnp.where(kt > kd, -1.0, 0.0)).astype(f32)
        aug = (q_aug * sign).astype(bf16)
        explicit_bias = jnp.where(kt == kd, 1.0, 0.0).astype(f32) * bias_ref[...]
        out = []
        for mi in range(2):
            s = _dot_nt(jnp.concatenate([q_main[mi], aug], axis=1), k) - explicit_bias
            s_ref[mi, kt] = s
            mx = running[mi]
            for c in range(lane_blocks):
                mx = jnp.maximum(mx, s[:, c * DA_V_DIM:(c + 1) * DA_V_DIM])
            out.append(mx)
        return tuple(out)

    span = 2 * tk
    reach = reach_ref[h]
    first_pair = jnp.maximum(qi * tq - reach, 0) // span
    n_pairs = jnp.minimum((qi * tq + tq - 1 + reach) // span + 1, nk // 2) - first_pair

    def score_pair(j, running):
        return scores(2 * j + 1, scores(2 * j, running))

    running = (jnp.full((tq, DA_V_DIM), -jnp.inf, f32),) * 2
    running = lax.fori_loop(
        0, n_pairs // 2,
        lambda i, r: score_pair(first_pair + 2 * i + 1, score_pair(first_pair + 2 * i, r)), running)
    running = lax.fori_loop(
        0, n_pairs % 2, lambda i, r: score_pair(first_pair + n_pairs - 1, r), running)
    if nk % 2:
        running = scores(nk - 1, running)
    row_max = [jnp.broadcast_to(jnp.max(mx, axis=-1, keepdims=True), (tq, DA_V_DIM)) for mx in running]

    acc_ref[...] = jnp.zeros_like(acc_ref)

    def weighted_values(mi, first, count):
        v = jnp.concatenate([v_ref[pl.ds(tile_start(first), count * tk), :], ones_col[:count * tk]], axis=1)
        ps = [jnp.exp(s_ref[mi, first + j, :, c * DA_V_DIM:(c + 1) * DA_V_DIM] - row_max[mi]).astype(bf16)
              for j in range(count) for c in range(lane_blocks)]
        return _dot(jnp.concatenate(ps, axis=1), v)

    def two_pairs(i, carry):
        j = first_pair + 2 * i
        for mi in range(2):
            acc_ref[mi] += weighted_values(mi, 2 * j, 2) + weighted_values(mi, 2 * j + 2, 2)
        return carry

    def last_pair(i, carry):
        for mi in range(2):
            acc_ref[mi] += weighted_values(mi, 2 * (first_pair + n_pairs - 1), 2)
        return carry

    lax.fori_loop(0, n_pairs // 2, two_pairs, 0)
    lax.fori_loop(0, n_pairs % 2, last_pair, 0)
    if nk % 2:
        for mi in range(2):
            acc_ref[mi] += weighted_values(mi, nk - 1, 1)

    lv = lam_ref[...]
    lam = (jnp.exp(jnp.sum(lv[0:1] * lv[1:2], axis=-1, keepdims=True))
           - jnp.exp(jnp.sum(lv[2:3] * lv[3:4], axis=-1, keepdims=True)) + lam_init)
    acc0 = acc_ref[0]
    acc1 = acc_ref[1]
    o = (acc0[:, :DA_V_DIM] / acc0[:, DA_V_DIM:DA_V_DIM + 1]
         - lam * (acc1[:, :DA_V_DIM] / acc1[:, DA_V_DIM:DA_V_DIM + 1]))
    ms = jnp.mean(o * o, axis=-1, keepdims=True)
    o_ref[...] = o * lax.rsqrt(ms + NORM_EPS) * subln_ref[...] * (1.0 - lam_init)


def _alibi_reach(q_norm, k_norm):
    gq = jnp.max(jnp.abs(q_norm))
    gk = jnp.max(jnp.abs(k_norm))
    slopes = 2.0 ** (-(ALIBI_MAX_BIAS / DA_HEADS) * jnp.arange(1, DA_HEADS + 1, dtype=f32))
    dist = (1.05 * 16.0 * gq * gk + EXP2_UNDERFLOW * math.log(2.0)) / slopes
    return jnp.minimum(jnp.ceil(dist), float(MAX_REACH)).astype(jnp.int32)


def _alibi_columns(seq_len):
    pos = jnp.arange(seq_len, dtype=jnp.int32)
    hi = (pos // POS_SPLIT).astype(f32)[None, :, None]
    lo = (pos % POS_SPLIT).astype(f32)[None, :, None]
    slopes = (2.0 ** (-(ALIBI_MAX_BIAS / DA_HEADS) * jnp.arange(1, DA_HEADS + 1, dtype=f32)))[:, None, None]
    col = jnp.arange(DA_V_DIM, dtype=jnp.int32)[None, None, :]
    q_cols = jnp.where(col == 0, -slopes * POS_SPLIT * hi,
                       jnp.where(col == 1, -slopes * lo, jnp.where(col < 4, 1.0, 0.0)))
    k_cols = jnp.where(col < 2, 1.0,
                       jnp.where(col == 2, slopes * POS_SPLIT * hi, jnp.where(col == 3, slopes * lo, 0.0)))
    return q_cols.astype(bf16), k_cols.astype(bf16)


def _flash(q0, q1, k, v, reach, lam_vec, subln, lam_init, batch, seq_len):
    T = q0.shape[0]
    tq = min(FLASH_TQ, seq_len)
    tk = min(FLASH_TK, seq_len)
    nq = seq_len // tq
    q_cols, k_cols = _alibi_columns(seq_len)
    offsets = tk // tq
    qk_dist = jnp.abs(jnp.arange(tq, dtype=jnp.int32)[None, :, None] - jnp.arange(tk, dtype=jnp.int32)[None, None, :]
                      + tq * jnp.arange(offsets, dtype=jnp.int32)[:, None, None]).astype(f32)
    slopes = 2.0 ** (-(ALIBI_MAX_BIAS / DA_HEADS) * jnp.arange(1, DA_HEADS + 1, dtype=f32))
    overlap_bias = slopes[:, None, None, None] * qk_dist[None]
    qspec = pl.BlockSpec((tq, DA_V_DIM), lambda b, h, i, reach_ref: (b * nq + i, h))
    kspec = pl.BlockSpec((seq_len, DA_V_DIM), lambda b, h, i, reach_ref: (b, h), pipeline_mode=pl.Buffered(1))
    return pl.pallas_call(
        functools.partial(_flash_body, lam_init, tq, tk, seq_len),
        grid_spec=pltpu.PrefetchScalarGridSpec(
            num_scalar_prefetch=1,
            grid=(batch, DA_HEADS, nq),
            in_specs=[qspec, qspec,
                      pl.BlockSpec((None, tq, DA_V_DIM), lambda b, h, i, reach_ref: (h, i, 0)),
                      kspec,
                      pl.BlockSpec((None, seq_len, DA_V_DIM), lambda b, h, i, reach_ref: (h, 0, 0),
                                   pipeline_mode=pl.Buffered(1)),
                      kspec,
                      pl.BlockSpec((None, None, tq, tk), lambda b, h, i, reach_ref: (h, i % offsets, 0, 0)),
                      pl.BlockSpec((4, HEAD_DIM), lambda b, h, i, reach_ref: (0, 0)),
                      pl.BlockSpec((1, DA_V_DIM), lambda b, h, i, reach_ref: (0, 0))],
            out_specs=pl.BlockSpec((tq, DA_V_DIM), lambda b, h, i, reach_ref: (b * nq + i, h)),
            scratch_shapes=[pltpu.VMEM((2, seq_len // tk, tq, tk), f32), pltpu.VMEM((2, tq, DA_EXT), f32)],
        ),
        out_shape=jax.ShapeDtypeStruct((T, DA_WIDTH), f32),
        compiler_params=pltpu.CompilerParams(
            dimension_semantics=("arbitrary", "arbitrary", "arbitrary"),
            vmem_limit_bytes=FLASH_VMEM_LIMIT),
        name="flash_diff_attn",
    )(reach, q0, q1, q_cols, k, k_cols, v, overlap_bias, lam_vec, subln)


def _out_body(h_ref, yf_ref, yb_ref, bonus_ref, grw_ref, gda_ref, o_ref, p_ref, seg_ref, lng_ref, lnb_ref,
              wout_ref, pproj_ref, pnorm_ref, gw_ref, gb_ref, out_ref):
    seg = seg_ref[...]
    inv_n = 1.0 / HEAD_DIM
    y = yf_ref[...] + yb_ref[...]
    mu = _segsum(y, seg) * inv_n
    yc = y - mu
    var = _segsum(yc * yc, seg) * inv_n
    y_rw = (yc * lax.rsqrt(var + LN_X_EPS) * lng_ref[...] + lnb_ref[...] + bonus_ref[...])
    y_rw = y_rw * _silu(grw_ref[...])
    y_da = o_ref[...] * _silu(gda_ref[...])
    h1 = (h_ref[...] + _dot(y_rw.astype(bf16), wout_ref[0:512, :])
          + _dot(y_da.astype(bf16), wout_ref[512:1024, :]))
    e = _dot(p_ref[...].astype(bf16), pproj_ref[...])
    e = e * lax.rsqrt(jnp.mean(e * e, axis=-1, keepdims=True) + NORM_EPS) * pnorm_ref[...]
    gate = _sigmoid(_dot(h1.astype(bf16), gw_ref[...]) + gb_ref[...])
    out_ref[...] = h1 + gate * e


def _out_stage(h, y_f, y_b, bonus, gates, o, p, prm):
    T = h.shape[0]
    tm = PROJ_TILE
    row = lambda w: pl.BlockSpec((tm, w), lambda i: (i, 0))
    const = lambda shape: pl.BlockSpec(shape, lambda i: (0,) * len(shape))
    return pl.pallas_call(
        _out_body,
        grid=(T // tm,),
        in_specs=[row(D_MODEL), row(512), row(512), row(512),
                  pl.BlockSpec((tm, 512), lambda i: (i, 0)),
                  pl.BlockSpec((tm, 512), lambda i: (i, 1)),
                  row(512), row(D_PLE),
                  const((512, 512)), const((1, 512)), const((1, 512)),
                  const((D_MODEL, D_MODEL)), const((D_PLE, D_MODEL)), const((1, D_MODEL)),
                  const((D_MODEL, D_MODEL)), const((1, D_MODEL))],
        out_specs=row(D_MODEL),
        out_shape=jax.ShapeDtypeStruct((T, D_MODEL), f32),
        compiler_params=pltpu.CompilerParams(
            dimension_semantics=("arbitrary",), vmem_limit_bytes=VMEM_LIMIT),
        name="out_stage",
    )(h, y_f, y_b, bonus, gates, gates, o, p, prm["seg"], prm["ln_g"], prm["ln_b"], prm["w_out"],
      prm["ple_proj"], prm["ple_norm"], prm["gate_w"], prm["gate_b"])


def _chunk_triangles(tm):
    t = np.arange(tm)
    same = (t[:, None] // CHUNK) == (t[None, :] // CHUNK)
    low_incl = same & (t[None, :] <= t[:, None])
    up_strict = same & (t[None, :] > t[:, None])
    up_incl = same & (t[None, :] >= t[:, None])
    low_strict = same & (t[None, :] < t[:, None])
    tri = np.stack([np.concatenate([low_incl, up_strict], 0), np.concatenate([up_incl, low_strict], 0)])
    return jnp.asarray(tri.astype(np.float32), dtype=bf16)


def _layer_params(i, norm_pre, w_in, w_out, rw_conv_w, rw_conv_b, rw_w0, rw_w_up, rw_a0, rw_a_up, rw_k_k,
                  rw_k_a, rw_r_k, rw_ln_g, rw_ln_b, da_q_norm, da_k_norm, da_lambda, da_subln, ple_proj,
                  ple_norm, ple_gate_w, ple_gate_b):
    pad = CONV_PAD - (CONV_COLS)
    w = w_in[i]
    o1 = CONV_COLS + RW_WIDTH
    o4 = o1 + 3 * DA_WIDTH
    w_pad = jnp.concatenate(
        [w[:, :CONV_COLS], jnp.zeros((D_MODEL, pad), f32), w[:, CONV_COLS:o1], w[:, o4:], w[:, o1:o4]],
        axis=1).astype(bf16)
    wup = rw_w_up[i]
    zeros = jnp.zeros((W_LORA, RW_WIDTH), f32)
    w_up_bd = jnp.concatenate(
        [jnp.concatenate([wup[0], zeros], 1), jnp.concatenate([zeros, wup[1]], 1)], 0).astype(bf16)
    a_up_pad = jnp.concatenate([rw_a_up[i], jnp.zeros((64, RW_WIDTH), f32)], 0).astype(bf16)
    lane = np.arange(512)
    seg = jnp.asarray((lane[:, None] // HEAD_DIM == lane[None, :] // HEAD_DIM).astype(np.float32), dtype=bf16)
    return dict(
        norm_pre=norm_pre[i].reshape(1, D_MODEL),
        w_in=w_pad,
        conv_w=jnp.pad(rw_conv_w[i], ((0, 0), (0, pad))),
        conv_b=jnp.pad(rw_conv_b[i], (0, pad)).reshape(1, CONV_PAD),
        w0=rw_w0[i].reshape(1, 2 * RW_WIDTH),
        w_up=w_up_bd,
        a0=rw_a0[i].reshape(1, RW_WIDTH),
        a_up=a_up_pad,
        k_k=rw_k_k[i].reshape(1, RW_WIDTH),
        k_a=rw_k_a[i].reshape(1, RW_WIDTH),
        r_k=rw_r_k[i].reshape(1, RW_WIDTH),
        ln_g=rw_ln_g[i].reshape(1, RW_WIDTH),
        ln_b=rw_ln_b[i].reshape(1, RW_WIDTH),
        q_norm=jnp.tile(da_q_norm[i].reshape(1, 2 * HEAD_DIM), (1, DA_HEADS)),
        k_norm=jnp.tile(da_k_norm[i].reshape(1, 2 * HEAD_DIM), (1, DA_HEADS)),
        reach=_alibi_reach(da_q_norm[i], da_k_norm[i]),
        lam_vec=da_lambda[i],
        subln=da_subln[i].reshape(1, DA_V_DIM),
        w_out=w_out[i].astype(bf16),
        ple_proj=ple_proj[i].astype(bf16),
        ple_norm=ple_norm[i].reshape(1, D_MODEL),
        gate_w=ple_gate_w[i].astype(bf16),
        gate_b=ple_gate_b[i].reshape(1, D_MODEL),
        seg=seg,
        tri=_chunk_triangles(TOKEN_TILE),
    )


def _layer(h, p, lam_init, prm, batch, seq_len):
    gates, q0, q1, k, v, slab, decay, v_rw, bonus = _front(h, seq_len, prm)
    y_f, y_b = _rwkv_scan(slab, decay, v_rw, batch, seq_len)
    o = _flash(q0, q1, k, v, prm["reach"], prm["lam_vec"], prm["subln"], lam_init, batch, seq_len)
    return _out_stage(h, y_f, y_b, bonus, gates, o, p, prm)


def _trunk(x, p, layers):
    batch, seq_len, _ = x.shape
    h = x.reshape(batch * seq_len, D_MODEL)
    for i, prm in enumerate(layers):
        lam_init = 0.8 - 0.6 * math.exp(-0.3 * i)
        h = _layer(h, p[i].reshape(batch * seq_len, D_PLE), lam_init, prm, batch, seq_len)
    return h.reshape(batch, seq_len, D_MODEL)


def kernel(x_prompt, x_sample, p_prompt, p_sample, norm_pre, w_in, w_out, rw_conv_w, rw_conv_b, rw_w0, rw_w_up, rw_a0, rw_a_up, rw_k_k, rw_k_a, rw_r_k, rw_ln_g, rw_ln_b, da_q_norm, da_k_norm, da_lambda, da_subln, ple_proj, ple_norm, ple_gate_w, ple_gate_b):
    depth = norm_pre.shape[0]
    layers = [_layer_params(i, norm_pre, w_in, w_out, rw_conv_w, rw_conv_b, rw_w0, rw_w_up, rw_a0, rw_a_up,
                            rw_k_k, rw_k_a, rw_r_k, rw_ln_g, rw_ln_b, da_q_norm, da_k_norm, da_lambda,
                            da_subln, ple_proj, ple_norm, ple_gate_w, ple_gate_b) for i in range(depth)]
    return (_trunk(x_prompt, p_prompt, layers), _trunk(x_sample, p_sample, layers))
```

```python
import functools
import math

import numpy as np
import jax
import jax.numpy as jnp
from jax import lax
from jax.experimental import pallas as pl
from jax.experimental.pallas import tpu as pltpu

f32 = jnp.float32
bf16 = jnp.bfloat16

D_MODEL = 1024
D_PLE = 256
RW_WIDTH = 512
HEAD_DIM = 64
W_LORA = 64
CONV_COLS = 3 * RW_WIDTH + 2 * W_LORA + 64
CONV_PAD = 1792
DA_HEADS = 4
DA_V_DIM = 128
DA_WIDTH = 512
DA_EXT = 256
GATE_COLS = 2 * 512
NORM_EPS = 1e-6
LN_X_EPS = 64e-5
KK_EPS = 1e-12
ALIBI_MAX_BIAS = 8.0
DECAY_SCALE = math.exp(-0.5)
EXP2_UNDERFLOW = 160.0
MAX_REACH = 1 << 24
POS_SPLIT = 128

CHUNK = 64
CHUNKS_PER_STEP = 4
GROUP = 256
HEADS_PER_GROUP = GROUP // HEAD_DIM
TOKEN_TILE = 256
PROJ_TILE = 512
FLASH_TQ = 512
FLASH_TK = 512
VMEM_LIMIT = 48 * 1024 * 1024
FLASH_VMEM_LIMIT = 56 * 1024 * 1024


def _dot(a, b):
    return jnp.dot(a, b, preferred_element_type=f32)


def _dot_nt(a, b):
    return lax.dot_general(a, b, (((1,), (1,)), ((), ())), preferred_element_type=f32)


def _split2(x):
    hi = x.astype(bf16)
    lo = (x - hi.astype(f32)).astype(bf16)
    return hi, lo


def _split3(x):
    hi = x.astype(bf16)
    r1 = x - hi.astype(f32)
    mid = r1.astype(bf16)
    lo = (r1 - mid.astype(f32)).astype(bf16)
    return hi, mid, lo


def _dot_exact01(m01, x, parts):
    ps = _split3(x) if parts == 3 else _split2(x)
    acc = _dot(m01, ps[0])
    for p in ps[1:]:
        acc = acc + _dot(m01, p)
    return acc


def _segsum(x, seg):
    hi, lo = _split2(x)
    return _dot(hi, seg) + _dot(lo, seg)


def _sigmoid(x):
    return 1.0 / (1.0 + jnp.exp(-x))


def _silu(x):
    return x * _sigmoid(x)


def _rwkv_chunk_operands(xm, prev_row, next_row, cw_ref, cb_ref, w0_ref, wup_ref, a0_ref, aup_ref, kk_ref,
                         ka_ref, rk_ref, seg, tri_ref, slab_ref, decay_ref, v_ref, bonus_ref):
    tm = xm.shape[0]
    row = lax.broadcasted_iota(jnp.int32, (tm, 1), 0)
    xp = jnp.where(row == 0, prev_row, pltpu.roll(xm, 1, 0))
    xn = jnp.where(row == tm - 1, next_row, pltpu.roll(xm, tm - 1, 0))
    cw = cw_ref[...]
    c = cb_ref[...] + xp * cw[0:1] + xm * cw[1:2] + xn * cw[2:3]

    r = c[:, 0:512]
    k = c[:, 512:1024]
    v = c[:, 1024:1536]
    zw = c[:, 1536:1664]
    za = c[:, 1664:1792]

    wl = w0_ref[...] + _dot(jnp.tanh(zw).astype(bf16), wup_ref[...])
    logw = -DECAY_SCALE * _sigmoid(wl)
    a = _sigmoid(a0_ref[...] + _dot(za.astype(bf16), aup_ref[...]))
    kk = k * kk_ref[...]
    kk = kk * lax.rsqrt(_segsum(kk * kk, seg) + KK_EPS)
    kmod = k * (1.0 + (a - 1.0) * ka_ref[...])
    kka = kk * a
    bonus_ref[...] = _segsum(r * kmod * rk_ref[...], seg) * v
    v_ref[...] = v.astype(bf16)

    for d in range(2):
        lw = logw[:, d * 512:(d + 1) * 512]
        cums = _dot_exact01(tri_ref[d], lw, 2)
        cum_in = cums[:tm]
        rev_ex = cums[tm:]
        e_pos = jnp.exp(cum_in)
        e_neg = jnp.exp(-cum_in)
        e_rev = jnp.exp(rev_ex)
        slab_ref[d, 0] = (r * e_pos).astype(bf16)
        slab_ref[d, 1] = (kmod * e_neg).astype(bf16)
        slab_ref[d, 2] = (-kk * jnp.exp(cum_in - lw)).astype(bf16)
        slab_ref[d, 3] = (kka * e_neg).astype(bf16)
        slab_ref[d, 4] = (kka * e_rev).astype(bf16)
        slab_ref[d, 5] = (kmod * e_rev).astype(bf16)
        total = jnp.exp(cum_in + rev_ex)
        for c in range(tm // CHUNK):
            decay_ref[d, c] = total[c * CHUNK:c * CHUNK + 1, :]


def _front_body(tiles_per_seq, n_tiles, x_ref, xnext_ref, g_ref, w_ref, qn_ref, kn_ref, seg_ref,
                cw_ref, cb_ref, w0_ref, wup_ref, a0_ref, aup_ref, kk_ref, ka_ref, rk_ref, tri_ref,
                gates_ref, q0_ref, q1_ref, k_ref, v_ref, slab_ref, decay_ref, vrw_ref, bonus_ref,
                conv_ref, last_row_ref):
    tm = x_ref.shape[0]
    i = pl.program_id(0)
    local = i % tiles_per_seq
    is_first = local == 0
    is_last = local == tiles_per_seq - 1
    g = g_ref[...]
    seg = seg_ref[...]

    def normed(x):
        ms = jnp.mean(x * x, axis=-1, keepdims=True)
        return (x * lax.rsqrt(ms + NORM_EPS) * g).astype(bf16)

    def project_conv(x, slot):
        un = normed(x)
        for c0 in range(0, CONV_PAD, 256):
            conv_ref[slot, :, c0:c0 + 256] = _dot(un, w_ref[:, c0:c0 + 256])

    @pl.when(i == 0)
    def _():
        conv_ref[1] = jnp.zeros((tm, CONV_PAD), f32)
        last_row_ref[...] = jnp.zeros_like(last_row_ref)
        project_conv(x_ref[...], 0)

    project_conv(xnext_ref[...], (i + 1) % 2)

    u = normed(x_ref[...])
    for c0 in range(0, GATE_COLS, 256):
        gates_ref[:, c0:c0 + 256] = _dot(u, w_ref[:, CONV_PAD + c0:CONV_PAD + c0 + 256])
    base = CONV_PAD + GATE_COLS
    q = _dot(u, w_ref[:, base:base + DA_WIDTH])
    k = _dot(u, w_ref[:, base + DA_WIDTH:base + 2 * DA_WIDTH])
    v_ref[...] = _dot(u, w_ref[:, base + 2 * DA_WIDTH:base + 3 * DA_WIDTH]).astype(bf16)
    inv_d = 1.0 / HEAD_DIM
    qn = q * lax.rsqrt(_segsum(q * q, seg) * inv_d + NORM_EPS) * qn_ref[...] * (HEAD_DIM ** -0.5)
    kn = k * lax.rsqrt(_segsum(k * k, seg) * inv_d + NORM_EPS) * kn_ref[...]
    lane = lax.broadcasted_iota(jnp.int32, (1, DA_WIDTH), 1)
    first_map = (lane // HEAD_DIM) % 2 == 0
    q0_ref[...] = jnp.where(first_map, qn, 0.0).astype(bf16)
    q1_ref[...] = jnp.where(first_map, 0.0, qn).astype(bf16)
    k_ref[...] = kn.astype(bf16)

    xm = conv_ref[i % 2]
    prev_row = jnp.where(is_first, 0.0, last_row_ref[...])
    next_row = jnp.where(is_last, 0.0, conv_ref[(i + 1) % 2, 0:1, :])
    last_row_ref[...] = xm[tm - 1:tm, :]
    _rwkv_chunk_operands(xm, prev_row, next_row, cw_ref, cb_ref, w0_ref, wup_ref, a0_ref, aup_ref, kk_ref,
                         ka_ref, rk_ref, seg, tri_ref, slab_ref, decay_ref, vrw_ref, bonus_ref)


def _front(h, seq_len, prm):
    T = h.shape[0]
    tm = TOKEN_TILE
    n_tiles = T // tm
    row = lambda w: pl.BlockSpec((tm, w), lambda i: (i, 0))
    const = lambda shape: pl.BlockSpec(shape, lambda i: (0,) * len(shape))
    attn = jax.ShapeDtypeStruct((T, DA_WIDTH), bf16)
    return pl.pallas_call(
        functools.partial(_front_body, seq_len // tm, n_tiles),
        grid=(n_tiles,),
        in_specs=[
            row(D_MODEL),
            pl.BlockSpec((tm, D_MODEL), lambda i: (jnp.minimum(i + 1, n_tiles - 1), 0)),
            const((1, D_MODEL)), const((D_MODEL, CONV_PAD + GATE_COLS + 3 * DA_WIDTH)),
            const((1, DA_WIDTH)), const((1, DA_WIDTH)), const((512, 512)),
            const((3, CONV_PAD)), const((1, CONV_PAD)), const((1, 1024)), const((128, 1024)),
            const((1, 512)), const((128, 512)), const((1, 512)), const((1, 512)), const((1, 512)),
            const((2, 2 * tm, tm)),
        ],
        out_specs=[
            row(GATE_COLS), row(DA_WIDTH), row(DA_WIDTH), row(DA_WIDTH), row(DA_WIDTH),
            pl.BlockSpec((2, 6, tm, 512), lambda i: (0, 0, i, 0)),
            pl.BlockSpec((2, tm // CHUNK, 1, 512), lambda i: (0, i, 0, 0)),
            row(512), row(512),
        ],
        out_shape=[
            jax.ShapeDtypeStruct((T, GATE_COLS), f32), attn, attn, attn, attn,
            jax.ShapeDtypeStruct((2, 6, T, 512), bf16),
            jax.ShapeDtypeStruct((2, T // CHUNK, 1, 512), f32),
            jax.ShapeDtypeStruct((T, 512), bf16),
            jax.ShapeDtypeStruct((T, 512), f32),
        ],
        scratch_shapes=[pltpu.VMEM((2, tm, CONV_PAD), f32), pltpu.VMEM((1, CONV_PAD), f32)],
        compiler_params=pltpu.CompilerParams(
            dimension_semantics=("arbitrary",), vmem_limit_bytes=VMEM_LIMIT),
        name="front",
    )(h, h, prm["norm_pre"], prm["w_in"], prm["q_norm"], prm["k_norm"], prm["seg"],
      prm["conv_w"], prm["conv_b"], prm["w0"], prm["w_up"], prm["a0"], prm["a_up"], prm["k_k"], prm["k_a"],
      prm["r_k"], prm["tri"])


def _mm(a, b):
    return _dot(a.astype(bf16), b.astype(bf16))


def _stack_heads(x, head_masks):
    x = x.astype(bf16)
    return jnp.concatenate([x * hm for hm in head_masks], axis=0)


def _rwkv_scan_body(chunks_per_step, slab_f_ref, slab_b_ref, decay_f_ref, decay_b_ref, v_f_ref, v_b_ref,
                    y_f_ref, y_b_ref, z_ref):
    L = CHUNK
    n_groups = RW_WIDTH // GROUP

    @pl.when(pl.program_id(1) == 0)
    def _():
        z_ref[...] = jnp.zeros_like(z_ref)

    t_row = lax.broadcasted_iota(jnp.int32, (L, GROUP), 0)
    t_col = lax.broadcasted_iota(jnp.int32, (L, GROUP), 1) % L
    eye_cat = jnp.where(t_row == t_col, 1.0, 0.0)
    strict = [(t_col < t_row).astype(f32), (t_col > t_row).astype(f32)]
    incl = [(t_col <= t_row).astype(f32), (t_col >= t_row).astype(f32)]
    rg = lax.broadcasted_iota(jnp.int32, (GROUP, GROUP), 0)
    cg = lax.broadcasted_iota(jnp.int32, (GROUP, GROUP), 1)
    blockdiag = ((rg // HEAD_DIM) == (cg // HEAD_DIM)).astype(f32)
    eye_g = rg == cg
    lane = lax.broadcasted_iota(jnp.int32, (1, GROUP), 1) // HEAD_DIM
    head_masks = [(lane == h).astype(bf16) for h in range(HEADS_PER_GROUP)]
    stack = lambda x: _stack_heads(x, head_masks)

    chains = []
    for d, (slab_ref, decay_ref, v_ref) in enumerate(
            ((slab_f_ref, decay_f_ref, v_f_ref), (slab_b_ref, decay_b_ref, v_b_ref))):
        for sub in (range(chunks_per_step) if d == 0 else reversed(range(chunks_per_step))):
            rows = slice(sub * L, (sub + 1) * L)
            for g in range(n_groups):
                lanes = slice(g * GROUP, (g + 1) * GROUP)
                chains.append(dict(
                    d=d, g=g, lanes=lanes, rows=rows,
                    r=slab_ref[0, rows, lanes], k=slab_ref[1, rows, lanes], a=slab_ref[2, rows, lanes],
                    b=slab_ref[3, rows, lanes], bp=slab_ref[4, rows, lanes], kp=slab_ref[5, rows, lanes],
                    pdiag=decay_ref[sub, :, lanes], v=v_ref[rows, lanes]))

    for ch in chains:
        d = ch["d"]
        ch["v_s"] = stack(ch["v"])
        gram = _dot_nt(jnp.concatenate([ch["a"], ch["r"]], axis=0),
                       jnp.concatenate([stack(ch["b"]), stack(ch["k"])], axis=0))
        ch["m_ab"] = gram[:L, :GROUP] * strict[d]
        ch["m_ak"] = gram[:L, GROUP:] * strict[d]
        ch["m_rb"] = gram[L:, :GROUP] * incl[d]
        ch["m_rk"] = gram[L:, GROUP:] * incl[d]
    for ch in chains:
        ch["t"] = eye_cat + ch["m_ab"]
        ch["q"] = _mm(ch["m_ab"], stack(ch["m_ab"]))
    for _ in range(4):
        for ch in chains:
            p = _mm(jnp.concatenate([ch["t"], ch["q"]], axis=0), stack(ch["q"]))
            ch["t"] = ch["t"] + p[:L]
            ch["q"] = p[L:]
    for ch in chains:
        ch["t"] = ch["t"] + _mm(ch["t"], stack(ch["q"]))
    for ch in chains:
        ch["w1"] = _mm(ch["m_ak"], ch["v_s"])
    for ch in chains:
        au = _mm(ch["t"], jnp.concatenate([stack(ch["a"]), stack(ch["w1"])], axis=1))
        ch["a_hat"] = au[:, :GROUP]
        ch["u_hat"] = au[:, GROUP:]
    for ch in chains:
        o1 = _mm(ch["m_rb"], jnp.concatenate([stack(ch["a_hat"]), stack(ch["u_hat"])], axis=1))
        ch["r_hat"] = ch["r"].astype(f32) + o1[:, :GROUP]
        ch["y_hat"] = o1[:, GROUP:] + _mm(ch["m_rk"], ch["v_s"])
    for ch in chains:
        ch["bk_t"] = jnp.concatenate([ch["bp"].astype(f32).T, ch["kp"].astype(f32).T], axis=1).astype(bf16)
        ch["ra"] = jnp.concatenate([ch["r_hat"], ch["a_hat"]], axis=0).astype(bf16)
        ch["decay_col"] = jnp.sum(jnp.where(eye_g, ch["pdiag"], 0.0), axis=1, keepdims=True)
    y_refs = (y_f_ref, y_b_ref)
    for ch in chains:
        z = z_ref[ch["d"], ch["g"]]
        rz = _dot(ch["ra"], z.astype(bf16))
        y_refs[ch["d"]][ch["rows"], ch["lanes"]] = rz[:L] + ch["y_hat"]
        u = (rz[L:] + ch["u_hat"]).astype(bf16)
        z_ref[ch["d"], ch["g"]] = (ch["decay_col"] * z
                                   + _dot(ch["bk_t"], jnp.concatenate([u, ch["v"]], axis=0)) * blockdiag)


def _rwkv_scan(slab, decay, v, batch, seq_len):
    T = v.shape[0]
    chunks_per_step = min(CHUNKS_PER_STEP, seq_len // CHUNK)
    rows = chunks_per_step * CHUNK
    steps = seq_len // rows
    fwd = lambda b, c: b * steps + c
    bwd = lambda b, c: b * steps + (steps - 1 - c)
    return pl.pallas_call(
        functools.partial(_rwkv_scan_body, chunks_per_step),
        grid=(batch, steps),
        in_specs=[
            pl.BlockSpec((None, 6, rows, 512), lambda b, c: (0, 0, fwd(b, c), 0)),
            pl.BlockSpec((None, 6, rows, 512), lambda b, c: (1, 0, bwd(b, c), 0)),
            pl.BlockSpec((None, chunks_per_step, 1, 512), lambda b, c: (0, fwd(b, c), 0, 0)),
            pl.BlockSpec((None, chunks_per_step, 1, 512), lambda b, c: (1, bwd(b, c), 0, 0)),
            pl.BlockSpec((rows, 512), lambda b, c: (fwd(b, c), 0)),
            pl.BlockSpec((rows, 512), lambda b, c: (bwd(b, c), 0)),
        ],
        out_specs=[
            pl.BlockSpec((rows, 512), lambda b, c: (fwd(b, c), 0)),
            pl.BlockSpec((rows, 512), lambda b, c: (bwd(b, c), 0)),
        ],
        out_shape=[jax.ShapeDtypeStruct((T, 512), f32), jax.ShapeDtypeStruct((T, 512), f32)],
        scratch_shapes=[pltpu.VMEM((2, RW_WIDTH // GROUP, GROUP, GROUP), f32)],
        compiler_params=pltpu.CompilerParams(
            dimension_semantics=("arbitrary", "arbitrary"), vmem_limit_bytes=VMEM_LIMIT),
        name="rwkv_scan",
    )(slab, slab, decay, decay, v, v)


def _flash_body(lam_init, tq, tk, seq_len, reach_ref, q0_ref, q1_ref, qaug_ref, k_ref, kaug_ref, v_ref,
                bias_ref, lam_ref, subln_ref, o_ref, s_ref, acc_ref):
    h = pl.program_id(1)
    qi = pl.program_id(2)
    nk = seq_len // tk
    lane_blocks = tk // DA_V_DIM

    kd = (qi * tq) // tk
    q_main = (q0_ref[...], q1_ref[...])
    q_aug = qaug_ref[...].astype(f32)
    ones_col = jnp.where(lax.broadcasted_iota(jnp.int32, (2 * tk, DA_V_DIM), 1) == 0, 1.0, 0.0).astype(bf16)

    def tile_start(kt):
        return kt * tk if isinstance(kt, int) else pl.multiple_of(kt * tk, tk)

    def scores(kt, running):
        start = tile_start(kt)
        k = jnp.concatenate([k_ref[pl.ds(start, tk), :], kaug_ref[pl.ds(start, tk), :]], axis=1)
        sign = jnp.where(kt < kd, 1.0, jnp.where(kt > kd, -1.0, 0.0)).astype(f32)
        aug = (q_aug * sign).astype(bf16)
        explicit_bias = jnp.where(kt == kd, 1.0, 0.0).astype(f32) * bias_ref[...]
        out = []
        for mi in range(2):
            s = _dot_nt(jnp.concatenate([q_main[mi], aug], axis=1), k) - explicit_bias
            s_ref[mi, kt] = s
            mx = running[mi]
            for c in range(lane_blocks):
                mx = jnp.maximum(mx, s[:, c * DA_V_DIM:(c + 1) * DA_V_DIM])
            out.append(mx)
        return tuple(out)

    span = 2 * tk
    reach = reach_ref[h]
    first_pair = jnp.maximum(qi * tq - reach, 0) // span
    n_pairs = jnp.minimum((qi * tq + tq - 1 + reach) // span + 1, nk // 2) - first_pair

    quads = n_pairs // 4
    duo_first = first_pair + 4 * quads
    duos = (n_pairs % 4) // 2
    solo = n_pairs % 2

    def score_pairs(j, count, running):
        for t in range(2 * count):
            running = scores(2 * j + t, running)
        return running

    running = (jnp.full((tq, DA_V_DIM), -jnp.inf, f32),) * 2
    running = lax.fori_loop(0, quads, lambda i, r: score_pairs(first_pair + 4 * i, 4, r), running)
    running = lax.fori_loop(0, duos, lambda i, r: score_pairs(duo_first, 2, r), running)
    running = lax.fori_loop(0, solo, lambda i, r: score_pairs(first_pair + n_pairs - 1, 1, r), running)
    if nk % 2:
        running = scores(nk - 1, running)
    row_max = [jnp.broadcast_to(jnp.max(mx, axis=-1, keepdims=True), (tq, DA_V_DIM)) for mx in running]

    acc_ref[...] = jnp.zeros_like(acc_ref)

    def weighted_values(mi, first, count):
        v = jnp.concatenate([v_ref[pl.ds(tile_start(first), count * tk), :], ones_col[:count * tk]], axis=1)
        ps = [jnp.exp(s_ref[mi, first + j, :, c * DA_V_DIM:(c + 1) * DA_V_DIM] - row_max[mi]).astype(bf16)
              for j in range(count) for c in range(lane_blocks)]
        return _dot(jnp.concatenate(ps, axis=1), v)

    def accumulate_pairs(j, count):
        for mi in range(2):
            update = weighted_values(mi, 2 * j, 2)
            for t in range(1, count):
                update = update + weighted_values(mi, 2 * (j + t), 2)
            acc_ref[mi] += update

    def acc_loop(trips, first, count):
        def body(i, carry):
            accumulate_pairs(first + count * i, count)
            return carry
        lax.fori_loop(0, trips, body, 0)

    acc_loop(quads, first_pair, 4)
    acc_loop(duos, duo_first, 2)
    acc_loop(solo, first_pair + n_pairs - 1, 1)
    if nk % 2:
        for mi in range(2):
            acc_ref[mi] += weighted_values(mi, nk - 1, 1)

    lv = lam_ref[...]
    lam = (jnp.exp(jnp.sum(lv[0:1] * lv[1:2], axis=-1, keepdims=True))
           - jnp.exp(jnp.sum(lv[2:3] * lv[3:4], axis=-1, keepdims=True)) + lam_init)
    acc0 = acc_ref[0]
    acc1 = acc_ref[1]
    o = (acc0[:, :DA_V_DIM] / acc0[:, DA_V_DIM:DA_V_DIM + 1]
         - lam * (acc1[:, :DA_V_DIM] / acc1[:, DA_V_DIM:DA_V_DIM + 1]))
    ms = jnp.mean(o * o, axis=-1, keepdims=True)
    o_ref[...] = o * lax.rsqrt(ms + NORM_EPS) * subln_ref[...] * (1.0 - lam_init)


def _alibi_reach(q_norm, k_norm):
    gq = jnp.max(jnp.abs(q_norm))
    gk = jnp.max(jnp.abs(k_norm))
    slopes = 2.0 ** (-(ALIBI_MAX_BIAS / DA_HEADS) * jnp.arange(1, DA_HEADS + 1, dtype=f32))
    dist = (1.05 * 16.0 * gq * gk + EXP2_UNDERFLOW * math.log(2.0)) / slopes
    return jnp.minimum(jnp.ceil(dist), float(MAX_REACH)).astype(jnp.int32)


def _alibi_columns(seq_len):
    pos = jnp.arange(seq_len, dtype=jnp.int32)
    hi = (pos // POS_SPLIT).astype(f32)[None, :, None]
    lo = (pos % POS_SPLIT).astype(f32)[None, :, None]
    slopes = (2.0 ** (-(ALIBI_MAX_BIAS / DA_HEADS) * jnp.arange(1, DA_HEADS + 1, dtype=f32)))[:, None, None]
    col = jnp.arange(DA_V_DIM, dtype=jnp.int32)[None, None, :]
    q_cols = jnp.where(col == 0, -slopes * POS_SPLIT * hi,
                       jnp.where(col == 1, -slopes * lo, jnp.where(col < 4, 1.0, 0.0)))
    k_cols = jnp.where(col < 2, 1.0,
                       jnp.where(col == 2, slopes * POS_SPLIT * hi, jnp.where(col == 3, slopes * lo, 0.0)))
    return q_cols.astype(bf16), k_cols.astype(bf16)


def _flash(q0, q1, k, v, reach, lam_vec, subln, lam_init, batch, seq_len):
    T = q0.shape[0]
    tq = min(FLASH_TQ, seq_len)
    tk = min(FLASH_TK, seq_len)
    nq = seq_len // tq
    q_cols, k_cols = _alibi_columns(seq_len)
    offsets = tk // tq
    qk_dist = jnp.abs(jnp.arange(tq, dtype=jnp.int32)[None, :, None] - jnp.arange(tk, dtype=jnp.int32)[None, None, :]
                      + tq * jnp.arange(offsets, dtype=jnp.int32)[:, None, None]).astype(f32)
    slopes = 2.0 ** (-(ALIBI_MAX_BIAS / DA_HEADS) * jnp.arange(1, DA_HEADS + 1, dtype=f32))
    overlap_bias = slopes[:, None, None, None] * qk_dist[None]
    qspec = pl.BlockSpec((tq, DA_V_DIM), lambda b, h, i, reach_ref: (b * nq + i, h))
    kspec = pl.BlockSpec((seq_len, DA_V_DIM), lambda b, h, i, reach_ref: (b, h), pipeline_mode=pl.Buffered(1))
    return pl.pallas_call(
        functools.partial(_flash_body, lam_init, tq, tk, seq_len),
        grid_spec=pltpu.PrefetchScalarGridSpec(
            num_scalar_prefetch=1,
            grid=(batch, DA_HEADS, nq),
            in_specs=[qspec, qspec,
                      pl.BlockSpec((None, tq, DA_V_DIM), lambda b, h, i, reach_ref: (h, i, 0)),
                      kspec,
                      pl.BlockSpec((None, seq_len, DA_V_DIM), lambda b, h, i, reach_ref: (h, 0, 0),
                                   pipeline_mode=pl.Buffered(1)),
                      kspec,
                      pl.BlockSpec((None, None, tq, tk), lambda b, h, i, reach_ref: (h, i % offsets, 0, 0)),
                      pl.BlockSpec((4, HEAD_DIM), lambda b, h, i, reach_ref: (0, 0)),
                      pl.BlockSpec((1, DA_V_DIM), lambda b, h, i, reach_ref: (0, 0))],
            out_specs=pl.BlockSpec((tq, DA_V_DIM), lambda b, h, i, reach_ref: (b * nq + i, h)),
            scratch_shapes=[pltpu.VMEM((2, seq_len // tk, tq, tk), f32), pltpu.VMEM((2, tq, DA_EXT), f32)],
        ),
        out_shape=jax.ShapeDtypeStruct((T, DA_WIDTH), f32),
        compiler_params=pltpu.CompilerParams(
            dimension_semantics=("arbitrary", "arbitrary", "arbitrary"),
            vmem_limit_bytes=FLASH_VMEM_LIMIT),
        name="flash_diff_attn",
    )(reach, q0, q1, q_cols, k, k_cols, v, overlap_bias, lam_vec, subln)


def _out_body(h_ref, yf_ref, yb_ref, bonus_ref, grw_ref, gda_ref, o_ref, p_ref, seg_ref, lng_ref, lnb_ref,
              wout_ref, pproj_ref, pnorm_ref, gw_ref, gb_ref, out_ref):
    seg = seg_ref[...]
    inv_n = 1.0 / HEAD_DIM
    y = yf_ref[...] + yb_ref[...]
    mu = _segsum(y, seg) * inv_n
    yc = y - mu
    var = _segsum(yc * yc, seg) * inv_n
    y_rw = (yc * lax.rsqrt(var + LN_X_EPS) * lng_ref[...] + lnb_ref[...] + bonus_ref[...])
    y_rw = y_rw * _silu(grw_ref[...])
    y_da = o_ref[...] * _silu(gda_ref[...])
    h1 = (h_ref[...] + _dot(y_rw.astype(bf16), wout_ref[0:512, :])
          + _dot(y_da.astype(bf16), wout_ref[512:1024, :]))
    e = _dot(p_ref[...].astype(bf16), pproj_ref[...])
    e = e * lax.rsqrt(jnp.mean(e * e, axis=-1, keepdims=True) + NORM_EPS) * pnorm_ref[...]
    gate = _sigmoid(_dot(h1.astype(bf16), gw_ref[...]) + gb_ref[...])
    out_ref[...] = h1 + gate * e


def _out_stage(h, y_f, y_b, bonus, gates, o, p, prm):
    T = h.shape[0]
    tm = PROJ_TILE
    row = lambda w: pl.BlockSpec((tm, w), lambda i: (i, 0))
    const = lambda shape: pl.BlockSpec(shape, lambda i: (0,) * len(shape))
    return pl.pallas_call(
        _out_body,
        grid=(T // tm,),
        in_specs=[row(D_MODEL), row(512), row(512), row(512),
                  pl.BlockSpec((tm, 512), lambda i: (i, 0)),
                  pl.BlockSpec((tm, 512), lambda i: (i, 1)),
                  row(512), row(D_PLE),
                  const((512, 512)), const((1, 512)), const((1, 512)),
                  const((D_MODEL, D_MODEL)), const((D_PLE, D_MODEL)), const((1, D_MODEL)),
                  const((D_MODEL, D_MODEL)), const((1, D_MODEL))],
        out_specs=row(D_MODEL),
        out_shape=jax.ShapeDtypeStruct((T, D_MODEL), f32),
        compiler_params=pltpu.CompilerParams(
            dimension_semantics=("arbitrary",), vmem_limit_bytes=VMEM_LIMIT),
        name="out_stage",
    )(h, y_f, y_b, bonus, gates, gates, o, p, prm["seg"], prm["ln_g"], prm["ln_b"], prm["w_out"],
      prm["ple_proj"], prm["ple_norm"], prm["gate_w"], prm["gate_b"])


def _chunk_triangles(tm):
    t = np.arange(tm)
    same = (t[:, None] // CHUNK) == (t[None, :] // CHUNK)
    low_incl = same & (t[None, :] <= t[:, None])
    up_strict = same & (t[None, :] > t[:, None])
    up_incl = same & (t[None, :] >= t[:, None])
    low_strict = same & (t[None, :] < t[:, None])
    tri = np.stack([np.concatenate([low_incl, up_strict], 0), np.concatenate([up_incl, low_strict], 0)])
    return jnp.asarray(tri.astype(np.float32), dtype=bf16)


def _layer_params(i, norm_pre, w_in, w_out, rw_conv_w, rw_conv_b, rw_w0, rw_w_up, rw_a0, rw_a_up, rw_k_k,
                  rw_k_a, rw_r_k, rw_ln_g, rw_ln_b, da_q_norm, da_k_norm, da_lambda, da_subln, ple_proj,
                  ple_norm, ple_gate_w, ple_gate_b):
    pad = CONV_PAD - (CONV_COLS)
    w = w_in[i]
    o1 = CONV_COLS + RW_WIDTH
    o4 = o1 + 3 * DA_WIDTH
    w_pad = jnp.concatenate(
        [w[:, :CONV_COLS], jnp.zeros((D_MODEL, pad), f32), w[:, CONV_COLS:o1], w[:, o4:], w[:, o1:o4]],
        axis=1).astype(bf16)
    wup = rw_w_up[i]
    zeros = jnp.zeros((W_LORA, RW_WIDTH), f32)
    w_up_bd = jnp.concatenate(
        [jnp.concatenate([wup[0], zeros], 1), jnp.concatenate([zeros, wup[1]], 1)], 0).astype(bf16)
    a_up_pad = jnp.concatenate([rw_a_up[i], jnp.zeros((64, RW_WIDTH), f32)], 0).astype(bf16)
    lane = np.arange(512)
    seg = jnp.asarray((lane[:, None] // HEAD_DIM == lane[None, :] // HEAD_DIM).astype(np.float32), dtype=bf16)
    return dict(
        norm_pre=norm_pre[i].reshape(1, D_MODEL),
        w_in=w_pad,
        conv_w=jnp.pad(rw_conv_w[i], ((0, 0), (0, pad))),
        conv_b=jnp.pad(rw_conv_b[i], (0, pad)).reshape(1, CONV_PAD),
        w0=rw_w0[i].reshape(1, 2 * RW_WIDTH),
        w_up=w_up_bd,
        a0=rw_a0[i].reshape(1, RW_WIDTH),
        a_up=a_up_pad,
        k_k=rw_k_k[i].reshape(1, RW_WIDTH),
        k_a=rw_k_a[i].reshape(1, RW_WIDTH),
        r_k=rw_r_k[i].reshape(1, RW_WIDTH),
        ln_g=rw_ln_g[i].reshape(1, RW_WIDTH),
        ln_b=rw_ln_b[i].reshape(1, RW_WIDTH),
        q_norm=jnp.tile(da_q_norm[i].reshape(1, 2 * HEAD_DIM), (1, DA_HEADS)),
        k_norm=jnp.tile(da_k_norm[i].reshape(1, 2 * HEAD_DIM), (1, DA_HEADS)),
        reach=_alibi_reach(da_q_norm[i], da_k_norm[i]),
        lam_vec=da_lambda[i],
        subln=da_subln[i].reshape(1, DA_V_DIM),
        w_out=w_out[i].astype(bf16),
        ple_proj=ple_proj[i].astype(bf16),
        ple_norm=ple_norm[i].reshape(1, D_MODEL),
        gate_w=ple_gate_w[i].astype(bf16),
        gate_b=ple_gate_b[i].reshape(1, D_MODEL),
        seg=seg,
        tri=_chunk_triangles(TOKEN_TILE),
    )


def _layer(h, p, lam_init, prm, batch, seq_len):
    gates, q0, q1, k, v, slab, decay, v_rw, bonus = _front(h, seq_len, prm)
    y_f, y_b = _rwkv_scan(slab, decay, v_rw, batch, seq_len)
    o = _flash(q0, q1, k, v, prm["reach"], prm["lam_vec"], prm["subln"], lam_init, batch, seq_len)
    return _out_stage(h, y_f, y_b, bonus, gates, o, p, prm)


def _trunk(x, p, layers):
    batch, seq_len, _ = x.shape
    h = x.reshape(batch * seq_len, D_MODEL)
    for i, prm in enumerate(layers):
        lam_init = 0.8 - 0.6 * math.exp(-0.3 * i)
        h = _layer(h, p[i].reshape(batch * seq_len, D_PLE), lam_init, prm, batch, seq_len)
    return h.reshape(batch, seq_len, D_MODEL)


def kernel(x_prompt, x_sample, p_prompt, p_sample, norm_pre, w_in, w_out, rw_conv_w, rw_conv_b, rw_w0, rw_w_up, rw_a0, rw_a_up, rw_k_k, rw_k_a, rw_r_k, rw_ln_g, rw_ln_b, da_q_norm, da_k_norm, da_lambda, da_subln, ple_proj, ple_norm, ple_gate_w, ple_gate_b):
    depth = norm_pre.shape[0]
    layers = [_layer_params(i, norm_pre, w_in, w_out, rw_conv_w, rw_conv_b, rw_w0, rw_w_up, rw_a0, rw_a_up,
                            rw_k_k, rw_k_a, rw_r_k, rw_ln_g, rw_ln_b, da_q_norm, da_k_norm, da_lambda,
                            da_subln, ple_proj, ple_norm, ple_gate_w, ple_gate_b) for i in range(depth)]
    return (_trunk(x_prompt, p_prompt, layers), _trunk(x_sample, p_sample, layers))
```

```python
import functools
import math

import numpy as np
import jax
import jax.numpy as jnp
from jax import lax
from jax.experimental import pallas as pl
from jax.experimental.pallas import tpu as pltpu

f32 = jnp.float32
bf16 = jnp.bfloat16

D_MODEL = 1024
D_PLE = 256
RW_WIDTH = 512
HEAD_DIM = 64
W_LORA = 64
CONV_COLS = 3 * RW_WIDTH + 2 * W_LORA + 64
CONV_PAD = 1792
DA_HEADS = 4
DA_V_DIM = 128
DA_WIDTH = 512
DA_EXT = 256
GATE_COLS = 2 * 512
NORM_EPS = 1e-6
LN_X_EPS = 64e-5
KK_EPS = 1e-12
ALIBI_MAX_BIAS = 8.0
DECAY_SCALE = math.exp(-0.5)
EXP2_UNDERFLOW = 160.0
MAX_REACH = 1 << 24
POS_SPLIT = 128

CHUNK = 64
CHUNKS_PER_STEP = 4
GROUP = 256
HEADS_PER_GROUP = GROUP // HEAD_DIM
TOKEN_TILE = 256
PROJ_TILE = 512
FLASH_TQ = 512
FLASH_TK = 512
VMEM_LIMIT = 48 * 1024 * 1024
FLASH_VMEM_LIMIT = 56 * 1024 * 1024


def _dot(a, b):
    return jnp.dot(a, b, preferred_element_type=f32)


def _dot_nt(a, b):
    return lax.dot_general(a, b, (((1,), (1,)), ((), ())), preferred_element_type=f32)


def _split2(x):
    hi = x.astype(bf16)
    lo = (x - hi.astype(f32)).astype(bf16)
    return hi, lo


def _dot_exact01(m01, x):
    hi, lo = _split2(x)
    return _dot(m01, hi) + _dot(m01, lo)


def _segsum(x, seg):
    hi, lo = _split2(x)
    return _dot(hi, seg) + _dot(lo, seg)


def _sigmoid(x):
    return 1.0 / (1.0 + jnp.exp(-x))


def _silu(x):
    return x * _sigmoid(x)


def _rwkv_chunk_operands(xm, prev_row, next_row, cw_ref, cb_ref, w0_ref, wup_ref, a0_ref, aup_ref, kk_ref,
                         ka_ref, rk_ref, seg, tri_ref, slab_ref, decay_ref, v_ref, bonus_ref):
    tm = xm.shape[0]
    row = lax.broadcasted_iota(jnp.int32, (tm, 1), 0)
    xp = jnp.where(row == 0, prev_row, pltpu.roll(xm, 1, 0))
    xn = jnp.where(row == tm - 1, next_row, pltpu.roll(xm, tm - 1, 0))
    cw = cw_ref[...]
    c = cb_ref[...] + xp * cw[0:1] + xm * cw[1:2] + xn * cw[2:3]

    r = c[:, 0:512]
    k = c[:, 512:1024]
    v = c[:, 1024:1536]
    zw = c[:, 1536:1664]
    za = c[:, 1664:1792]

    wl = w0_ref[...] + _dot(jnp.tanh(zw).astype(bf16), wup_ref[...])
    logw = -DECAY_SCALE * _sigmoid(wl)
    a = _sigmoid(a0_ref[...] + _dot(za.astype(bf16), aup_ref[...]))
    kk = k * kk_ref[...]
    kk = kk * lax.rsqrt(_segsum(kk * kk, seg) + KK_EPS)
    kmod = k * (1.0 + (a - 1.0) * ka_ref[...])
    kka = kk * a
    bonus_ref[...] = _segsum(r * kmod * rk_ref[...], seg) * v
    v_ref[...] = v.astype(bf16)

    for d in range(2):
        lw = logw[:, d * 512:(d + 1) * 512]
        cums = _dot_exact01(tri_ref[d], lw)
        cum_in = cums[:tm]
        rev_ex = cums[tm:]
        e_pos = jnp.exp(cum_in)
        e_neg = jnp.exp(-cum_in)
        e_rev = jnp.exp(rev_ex)
        slab_ref[d, 0] = (r * e_pos).astype(bf16)
        slab_ref[d, 1] = (kmod * e_neg).astype(bf16)
        slab_ref[d, 2] = (-kk * jnp.exp(cum_in - lw)).astype(bf16)
        slab_ref[d, 3] = (kka * e_neg).astype(bf16)
        slab_ref[d, 4] = (kka * e_rev).astype(bf16)
        slab_ref[d, 5] = (kmod * e_rev).astype(bf16)
        total = jnp.exp(cum_in + rev_ex)
        for c in range(tm // CHUNK):
            decay_ref[d, c] = total[c * CHUNK:c * CHUNK + 1, :]


def _front_body(tiles_per_seq, n_tiles, x_ref, xnext_ref, g_ref, w_ref, qn_ref, kn_ref, seg_ref,
                cw_ref, cb_ref, w0_ref, wup_ref, a0_ref, aup_ref, kk_ref, ka_ref, rk_ref, tri_ref,
                gates_ref, q0_ref, q1_ref, k_ref, v_ref, slab_ref, decay_ref, vrw_ref, bonus_ref,
                conv_ref, last_row_ref):
    tm = x_ref.shape[0]
    i = pl.program_id(0)
    local = i % tiles_per_seq
    is_first = local == 0
    is_last = local == tiles_per_seq - 1
    g = g_ref[...]
    seg = seg_ref[...]

    def normed(x):
        ms = jnp.mean(x * x, axis=-1, keepdims=True)
        return (x * lax.rsqrt(ms + NORM_EPS) * g).astype(bf16)

    def project_conv(x, slot):
        un = normed(x)
        for c0 in range(0, CONV_PAD, 256):
            conv_ref[slot, :, c0:c0 + 256] = _dot(un, w_ref[:, c0:c0 + 256])

    @pl.when(i == 0)
    def _():
        conv_ref[1] = jnp.zeros((tm, CONV_PAD), f32)
        last_row_ref[...] = jnp.zeros_like(last_row_ref)
        project_conv(x_ref[...], 0)

    project_conv(xnext_ref[...], (i + 1) % 2)

    u = normed(x_ref[...])
    for c0 in range(0, GATE_COLS, 256):
        gates_ref[:, c0:c0 + 256] = _dot(u, w_ref[:, CONV_PAD + c0:CONV_PAD + c0 + 256])
    base = CONV_PAD + GATE_COLS
    q = _dot(u, w_ref[:, base:base + DA_WIDTH])
    k = _dot(u, w_ref[:, base + DA_WIDTH:base + 2 * DA_WIDTH])
    v_ref[...] = _dot(u, w_ref[:, base + 2 * DA_WIDTH:base + 3 * DA_WIDTH]).astype(bf16)
    inv_d = 1.0 / HEAD_DIM
    qn = q * lax.rsqrt(_segsum(q * q, seg) * inv_d + NORM_EPS) * qn_ref[...] * (HEAD_DIM ** -0.5)
    kn = k * lax.rsqrt(_segsum(k * k, seg) * inv_d + NORM_EPS) * kn_ref[...]
    lane = lax.broadcasted_iota(jnp.int32, (1, DA_WIDTH), 1)
    first_map = (lane // HEAD_DIM) % 2 == 0
    q0_ref[...] = jnp.where(first_map, qn, 0.0).astype(bf16)
    q1_ref[...] = jnp.where(first_map, 0.0, qn).astype(bf16)
    k_ref[...] = kn.astype(bf16)

    xm = conv_ref[i % 2]
    prev_row = jnp.where(is_first, 0.0, last_row_ref[...])
    next_row = jnp.where(is_last, 0.0, conv_ref[(i + 1) % 2, 0:1, :])
    last_row_ref[...] = xm[tm - 1:tm, :]
    _rwkv_chunk_operands(xm, prev_row, next_row, cw_ref, cb_ref, w0_ref, wup_ref, a0_ref, aup_ref, kk_ref,
                         ka_ref, rk_ref, seg, tri_ref, slab_ref, decay_ref, vrw_ref, bonus_ref)


def _front(h, seq_len, prm):
    T = h.shape[0]
    tm = TOKEN_TILE
    n_tiles = T // tm
    row = lambda w: pl.BlockSpec((tm, w), lambda i: (i, 0))
    const = lambda shape: pl.BlockSpec(shape, lambda i: (0,) * len(shape))
    attn = jax.ShapeDtypeStruct((T, DA_WIDTH), bf16)
    return pl.pallas_call(
        functools.partial(_front_body, seq_len // tm, n_tiles),
        grid=(n_tiles,),
        in_specs=[
            row(D_MODEL),
            pl.BlockSpec((tm, D_MODEL), lambda i: (jnp.minimum(i + 1, n_tiles - 1), 0)),
            const((1, D_MODEL)), const((D_MODEL, CONV_PAD + GATE_COLS + 3 * DA_WIDTH)),
            const((1, DA_WIDTH)), const((1, DA_WIDTH)), const((512, 512)),
            const((3, CONV_PAD)), const((1, CONV_PAD)), const((1, 1024)), const((128, 1024)),
            const((1, 512)), const((128, 512)), const((1, 512)), const((1, 512)), const((1, 512)),
            const((2, 2 * tm, tm)),
        ],
        out_specs=[
            row(GATE_COLS), row(DA_WIDTH), row(DA_WIDTH), row(DA_WIDTH), row(DA_WIDTH),
            pl.BlockSpec((2, 6, tm, 512), lambda i: (0, 0, i, 0)),
            pl.BlockSpec((2, tm // CHUNK, 1, 512), lambda i: (0, i, 0, 0)),
            row(512), row(512),
        ],
        out_shape=[
            jax.ShapeDtypeStruct((T, GATE_COLS), f32), attn, attn, attn, attn,
            jax.ShapeDtypeStruct((2, 6, T, 512), bf16),
            jax.ShapeDtypeStruct((2, T // CHUNK, 1, 512), f32),
            jax.ShapeDtypeStruct((T, 512), bf16),
            jax.ShapeDtypeStruct((T, 512), f32),
        ],
        scratch_shapes=[pltpu.VMEM((2, tm, CONV_PAD), f32), pltpu.VMEM((1, CONV_PAD), f32)],
        compiler_params=pltpu.CompilerParams(
            dimension_semantics=("arbitrary",), vmem_limit_bytes=VMEM_LIMIT),
        name="front",
    )(h, h, prm["norm_pre"], prm["w_in"], prm["q_norm"], prm["k_norm"], prm["seg"],
      prm["conv_w"], prm["conv_b"], prm["w0"], prm["w_up"], prm["a0"], prm["a_up"], prm["k_k"], prm["k_a"],
      prm["r_k"], prm["tri"])


def _mm(a, b):
    return _dot(a.astype(bf16), b.astype(bf16))


def _stack_heads(x, head_masks):
    x = x.astype(bf16)
    return jnp.concatenate([x * hm for hm in head_masks], axis=0)


def _rwkv_scan_body(chunks_per_step, slab_f_ref, slab_b_ref, decay_f_ref, decay_b_ref, v_f_ref, v_b_ref,
                    y_f_ref, y_b_ref, z_ref):
    L = CHUNK
    n_groups = RW_WIDTH // GROUP

    @pl.when(pl.program_id(1) == 0)
    def _():
        z_ref[...] = jnp.zeros_like(z_ref)

    t_row = lax.broadcasted_iota(jnp.int32, (L, GROUP), 0)
    t_col = lax.broadcasted_iota(jnp.int32, (L, GROUP), 1) % L
    eye_cat = jnp.where(t_row == t_col, 1.0, 0.0)
    strict = [(t_col < t_row).astype(f32), (t_col > t_row).astype(f32)]
    incl = [(t_col <= t_row).astype(f32), (t_col >= t_row).astype(f32)]
    rg = lax.broadcasted_iota(jnp.int32, (GROUP, GROUP), 0)
    cg = lax.broadcasted_iota(jnp.int32, (GROUP, GROUP), 1)
    blockdiag = ((rg // HEAD_DIM) == (cg // HEAD_DIM)).astype(f32)
    eye_g = rg == cg
    lane = lax.broadcasted_iota(jnp.int32, (1, GROUP), 1) // HEAD_DIM
    head_masks = [(lane == h).astype(bf16) for h in range(HEADS_PER_GROUP)]
    stack = lambda x: _stack_heads(x, head_masks)

    chains = []
    for d, (slab_ref, decay_ref, v_ref) in enumerate(
            ((slab_f_ref, decay_f_ref, v_f_ref), (slab_b_ref, decay_b_ref, v_b_ref))):
        for sub in (range(chunks_per_step) if d == 0 else reversed(range(chunks_per_step))):
            rows = slice(sub * L, (sub + 1) * L)
            for g in range(n_groups):
                lanes = slice(g * GROUP, (g + 1) * GROUP)
                chains.append(dict(
                    d=d, g=g, lanes=lanes, rows=rows,
                    r=slab_ref[0, rows, lanes], k=slab_ref[1, rows, lanes], a=slab_ref[2, rows, lanes],
                    b=slab_ref[3, rows, lanes], bp=slab_ref[4, rows, lanes], kp=slab_ref[5, rows, lanes],
                    pdiag=decay_ref[sub, :, lanes], v=v_ref[rows, lanes]))

    for ch in chains:
        d = ch["d"]
        ch["v_s"] = stack(ch["v"])
        gram = _dot_nt(jnp.concatenate([ch["a"], ch["r"]], axis=0),
                       jnp.concatenate([stack(ch["b"]), stack(ch["k"])], axis=0))
        ch["m_ab"] = gram[:L, :GROUP] * strict[d]
        ch["m_ak"] = gram[:L, GROUP:] * strict[d]
        ch["m_rb"] = gram[L:, :GROUP] * incl[d]
        ch["m_rk"] = gram[L:, GROUP:] * incl[d]
    for ch in chains:
        ch["t"] = eye_cat + ch["m_ab"]
        ch["q"] = _mm(ch["m_ab"], stack(ch["m_ab"]))
    for _ in range(4):
        for ch in chains:
            p = _mm(jnp.concatenate([ch["t"], ch["q"]], axis=0), stack(ch["q"]))
            ch["t"] = ch["t"] + p[:L]
            ch["q"] = p[L:]
    for ch in chains:
        ch["t"] = ch["t"] + _mm(ch["t"], stack(ch["q"]))
    for ch in chains:
        ch["w1"] = _mm(ch["m_ak"], ch["v_s"])
    for ch in chains:
        au = _mm(ch["t"], jnp.concatenate([stack(ch["a"]), stack(ch["w1"])], axis=1))
        ch["a_hat"] = au[:, :GROUP]
        ch["u_hat"] = au[:, GROUP:]
    for ch in chains:
        o1 = _mm(ch["m_rb"], jnp.concatenate([stack(ch["a_hat"]), stack(ch["u_hat"])], axis=1))
        ch["r_hat"] = ch["r"].astype(f32) + o1[:, :GROUP]
        ch["y_hat"] = o1[:, GROUP:] + _mm(ch["m_rk"], ch["v_s"])
    for ch in chains:
        ch["bk_t"] = jnp.concatenate([ch["bp"].astype(f32).T, ch["kp"].astype(f32).T], axis=1).astype(bf16)
        ch["ra"] = jnp.concatenate([ch["r_hat"], ch["a_hat"]], axis=0).astype(bf16)
        ch["decay_col"] = jnp.sum(jnp.where(eye_g, ch["pdiag"], 0.0), axis=1, keepdims=True)
    y_refs = (y_f_ref, y_b_ref)
    for ch in chains:
        z = z_ref[ch["d"], ch["g"]]
        rz = _dot(ch["ra"], z.astype(bf16))
        y_refs[ch["d"]][ch["rows"], ch["lanes"]] = rz[:L] + ch["y_hat"]
        u = (rz[L:] + ch["u_hat"]).astype(bf16)
        z_ref[ch["d"], ch["g"]] = (ch["decay_col"] * z
                                   + _dot(ch["bk_t"], jnp.concatenate([u, ch["v"]], axis=0)) * blockdiag)


def _rwkv_scan(slab, decay, v, batch, seq_len):
    T = v.shape[0]
    chunks_per_step = min(CHUNKS_PER_STEP, seq_len // CHUNK)
    rows = chunks_per_step * CHUNK
    steps = seq_len // rows
    fwd = lambda b, c: b * steps + c
    bwd = lambda b, c: b * steps + (steps - 1 - c)
    return pl.pallas_call(
        functools.partial(_rwkv_scan_body, chunks_per_step),
        grid=(batch, steps),
        in_specs=[
            pl.BlockSpec((None, 6, rows, 512), lambda b, c: (0, 0, fwd(b, c), 0)),
            pl.BlockSpec((None, 6, rows, 512), lambda b, c: (1, 0, bwd(b, c), 0)),
            pl.BlockSpec((None, chunks_per_step, 1, 512), lambda b, c: (0, fwd(b, c), 0, 0)),
            pl.BlockSpec((None, chunks_per_step, 1, 512), lambda b, c: (1, bwd(b, c), 0, 0)),
            pl.BlockSpec((rows, 512), lambda b, c: (fwd(b, c), 0)),
            pl.BlockSpec((rows, 512), lambda b, c: (bwd(b, c), 0)),
        ],
        out_specs=[
            pl.BlockSpec((rows, 512), lambda b, c: (fwd(b, c), 0)),
            pl.BlockSpec((rows, 512), lambda b, c: (bwd(b, c), 0)),
        ],
        out_shape=[jax.ShapeDtypeStruct((T, 512), f32), jax.ShapeDtypeStruct((T, 512), f32)],
        scratch_shapes=[pltpu.VMEM((2, RW_WIDTH // GROUP, GROUP, GROUP), f32)],
        compiler_params=pltpu.CompilerParams(
            dimension_semantics=("arbitrary", "arbitrary"), vmem_limit_bytes=VMEM_LIMIT),
        name="rwkv_scan",
    )(slab, slab, decay, decay, v, v)


def _flash_body(lam_init, tq, tk, seq_len, reach_ref, q0_ref, q1_ref, qaug_ref, k_ref, kaug_ref, v_ref,
                bias_ref, lam_ref, subln_ref, o_ref, s_ref, acc_ref):
    h = pl.program_id(1)
    qi = pl.program_id(2)
    nk = seq_len // tk
    lane_blocks = tk // DA_V_DIM

    kd = (qi * tq) // tk
    q_main = (q0_ref[...], q1_ref[...])
    q_aug = qaug_ref[...].astype(f32)
    ones_col = jnp.where(lax.broadcasted_iota(jnp.int32, (2 * tk, DA_V_DIM), 1) == 0, 1.0, 0.0).astype(bf16)

    def tile_start(kt):
        return kt * tk if isinstance(kt, int) else pl.multiple_of(kt * tk, tk)

    def scores(kt, running):
        start = tile_start(kt)
        k = jnp.concatenate([k_ref[pl.ds(start, tk), :], kaug_ref[pl.ds(start, tk), :]], axis=1)
        sign = jnp.where(kt < kd, 1.0, jnp.where(kt > kd, -1.0, 0.0)).astype(f32)
        aug = (q_aug * sign).astype(bf16)
        explicit_bias = jnp.where(kt == kd, 1.0, 0.0).astype(f32) * bias_ref[...]
        out = []
        for mi in range(2):
            s = _dot_nt(jnp.concatenate([q_main[mi], aug], axis=1), k) - explicit_bias
            s_ref[mi, kt] = s
            mx = running[mi]
            for c in range(lane_blocks):
                mx = jnp.maximum(mx, s[:, c * DA_V_DIM:(c + 1) * DA_V_DIM])
            out.append(mx)
        return tuple(out)

    span = 2 * tk
    reach = reach_ref[h]
    first_pair = jnp.maximum(qi * tq - reach, 0) // span
    n_pairs = jnp.minimum((qi * tq + tq - 1 + reach) // span + 1, nk // 2) - first_pair

    quads = n_pairs // 4
    duo_first = first_pair + 4 * quads
    duos = (n_pairs % 4) // 2
    solo = n_pairs % 2

    def score_pairs(j, count, running):
        for t in range(2 * count):
            running = scores(2 * j + t, running)
        return running

    running = (jnp.full((tq, DA_V_DIM), -jnp.inf, f32),) * 2
    running = lax.fori_loop(0, quads, lambda i, r: score_pairs(first_pair + 4 * i, 4, r), running)
    running = lax.fori_loop(0, duos, lambda i, r: score_pairs(duo_first, 2, r), running)
    running = lax.fori_loop(0, solo, lambda i, r: score_pairs(first_pair + n_pairs - 1, 1, r), running)
    if nk % 2:
        running = scores(nk - 1, running)
    row_max = [jnp.broadcast_to(jnp.max(mx, axis=-1, keepdims=True), (tq, DA_V_DIM)) for mx in running]

    acc_ref[...] = jnp.zeros_like(acc_ref)

    def weighted_values(mi, first, count):
        v = jnp.concatenate([v_ref[pl.ds(tile_start(first), count * tk), :], ones_col[:count * tk]], axis=1)
        ps = [jnp.exp(s_ref[mi, first + j, :, c * DA_V_DIM:(c + 1) * DA_V_DIM] - row_max[mi]).astype(bf16)
              for j in range(count) for c in range(lane_blocks)]
        return _dot(jnp.concatenate(ps, axis=1), v)

    def accumulate_pairs(j, count):
        for mi in range(2):
            update = weighted_values(mi, 2 * j, 2)
            for t in range(1, count):
                update = update + weighted_values(mi, 2 * (j + t), 2)
            acc_ref[mi] += update

    def acc_loop(trips, first, count):
        def body(i, carry):
            accumulate_pairs(first + count * i, count)
            return carry
        lax.fori_loop(0, trips, body, 0)

    acc_loop(quads, first_pair, 4)
    acc_loop(duos, duo_first, 2)
    acc_loop(solo, first_pair + n_pairs - 1, 1)
    if nk % 2:
        for mi in range(2):
            acc_ref[mi] += weighted_values(mi, nk - 1, 1)

    lv = lam_ref[...]
    lam = (jnp.exp(jnp.sum(lv[0:1] * lv[1:2], axis=-1, keepdims=True))
           - jnp.exp(jnp.sum(lv[2:3] * lv[3:4], axis=-1, keepdims=True)) + lam_init)
    acc0 = acc_ref[0]
    acc1 = acc_ref[1]
    o = (acc0[:, :DA_V_DIM] / acc0[:, DA_V_DIM:DA_V_DIM + 1]
         - lam * (acc1[:, :DA_V_DIM] / acc1[:, DA_V_DIM:DA_V_DIM + 1]))
    ms = jnp.mean(o * o, axis=-1, keepdims=True)
    o_ref[...] = o * lax.rsqrt(ms + NORM_EPS) * subln_ref[...] * (1.0 - lam_init)


def _alibi_reach(q_norm, k_norm):
    gq = jnp.max(jnp.abs(q_norm))
    gk = jnp.max(jnp.abs(k_norm))
    slopes = 2.0 ** (-(ALIBI_MAX_BIAS / DA_HEADS) * jnp.arange(1, DA_HEADS + 1, dtype=f32))
    dist = (1.05 * 16.0 * gq * gk + EXP2_UNDERFLOW * math.log(2.0)) / slopes
    return jnp.minimum(jnp.ceil(dist), float(MAX_REACH)).astype(jnp.int32)


def _alibi_columns(seq_len):
    pos = jnp.arange(seq_len, dtype=jnp.int32)
    hi = (pos // POS_SPLIT).astype(f32)[None, :, None]
    lo = (pos % POS_SPLIT).astype(f32)[None, :, None]
    slopes = (2.0 ** (-(ALIBI_MAX_BIAS / DA_HEADS) * jnp.arange(1, DA_HEADS + 1, dtype=f32)))[:, None, None]
    col = jnp.arange(DA_V_DIM, dtype=jnp.int32)[None, None, :]
    q_cols = jnp.where(col == 0, -slopes * POS_SPLIT * hi,
                       jnp.where(col == 1, -slopes * lo, jnp.where(col < 4, 1.0, 0.0)))
    k_cols = jnp.where(col < 2, 1.0,
                       jnp.where(col == 2, slopes * POS_SPLIT * hi, jnp.where(col == 3, slopes * lo, 0.0)))
    return q_cols.astype(bf16), k_cols.astype(bf16)


def _flash(q0, q1, k, v, reach, lam_vec, subln, lam_init, batch, seq_len):
    T = q0.shape[0]
    tq = min(FLASH_TQ, seq_len)
    tk = min(FLASH_TK, seq_len)
    nq = seq_len // tq
    q_cols, k_cols = _alibi_columns(seq_len)
    offsets = tk // tq
    qk_dist = jnp.abs(jnp.arange(tq, dtype=jnp.int32)[None, :, None] - jnp.arange(tk, dtype=jnp.int32)[None, None, :]
                      + tq * jnp.arange(offsets, dtype=jnp.int32)[:, None, None]).astype(f32)
    slopes = 2.0 ** (-(ALIBI_MAX_BIAS / DA_HEADS) * jnp.arange(1, DA_HEADS + 1, dtype=f32))
    overlap_bias = slopes[:, None, None, None] * qk_dist[None]
    qspec = pl.BlockSpec((tq, DA_V_DIM), lambda b, h, i, reach_ref: (b * nq + i, h))
    kspec = pl.BlockSpec((seq_len, DA_V_DIM), lambda b, h, i, reach_ref: (b, h), pipeline_mode=pl.Buffered(1))
    return pl.pallas_call(
        functools.partial(_flash_body, lam_init, tq, tk, seq_len),
        grid_spec=pltpu.PrefetchScalarGridSpec(
            num_scalar_prefetch=1,
            grid=(batch, DA_HEADS, nq),
            in_specs=[qspec, qspec,
                      pl.BlockSpec((None, tq, DA_V_DIM), lambda b, h, i, reach_ref: (h, i, 0)),
                      kspec,
                      pl.BlockSpec((None, seq_len, DA_V_DIM), lambda b, h, i, reach_ref: (h, 0, 0),
                                   pipeline_mode=pl.Buffered(1)),
                      kspec,
                      pl.BlockSpec((None, None, tq, tk), lambda b, h, i, reach_ref: (h, i % offsets, 0, 0)),
                      pl.BlockSpec((4, HEAD_DIM), lambda b, h, i, reach_ref: (0, 0)),
                      pl.BlockSpec((1, DA_V_DIM), lambda b, h, i, reach_ref: (0, 0))],
            out_specs=pl.BlockSpec((tq, DA_V_DIM), lambda b, h, i, reach_ref: (b * nq + i, h)),
            scratch_shapes=[pltpu.VMEM((2, seq_len // tk, tq, tk), f32), pltpu.VMEM((2, tq, DA_EXT), f32)],
        ),
        out_shape=jax.ShapeDtypeStruct((T, DA_WIDTH), f32),
        compiler_params=pltpu.CompilerParams(
            dimension_semantics=("arbitrary", "arbitrary", "arbitrary"),
            vmem_limit_bytes=FLASH_VMEM_LIMIT),
        name="flash_diff_attn",
    )(reach, q0, q1, q_cols, k, k_cols, v, overlap_bias, lam_vec, subln)


def _out_body(h_ref, yf_ref, yb_ref, bonus_ref, grw_ref, gda_ref, o_ref, p_ref, seg_ref, lng_ref, lnb_ref,
              wout_ref, pproj_ref, pnorm_ref, gw_ref, gb_ref, out_ref):
    seg = seg_ref[...]
    inv_n = 1.0 / HEAD_DIM
    y = yf_ref[...] + yb_ref[...]
    mu = _segsum(y, seg) * inv_n
    yc = y - mu
    var = _segsum(yc * yc, seg) * inv_n
    y_rw = (yc * lax.rsqrt(var + LN_X_EPS) * lng_ref[...] + lnb_ref[...] + bonus_ref[...])
    y_rw = y_rw * _silu(grw_ref[...])
    y_da = o_ref[...] * _silu(gda_ref[...])
    h1 = (h_ref[...] + _dot(y_rw.astype(bf16), wout_ref[0:512, :])
          + _dot(y_da.astype(bf16), wout_ref[512:1024, :]))
    e = _dot(p_ref[...].astype(bf16), pproj_ref[...])
    e = e * lax.rsqrt(jnp.mean(e * e, axis=-1, keepdims=True) + NORM_EPS) * pnorm_ref[...]
    gate = _sigmoid(_dot(h1.astype(bf16), gw_ref[...]) + gb_ref[...])
    out_ref[...] = h1 + gate * e


def _out_stage(h, y_f, y_b, bonus, gates, o, p, prm):
    T = h.shape[0]
    tm = PROJ_TILE
    row = lambda w: pl.BlockSpec((tm, w), lambda i: (i, 0))
    const = lambda shape: pl.BlockSpec(shape, lambda i: (0,) * len(shape))
    return pl.pallas_call(
        _out_body,
        grid=(T // tm,),
        in_specs=[row(D_MODEL), row(512), row(512), row(512),
                  pl.BlockSpec((tm, 512), lambda i: (i, 0)),
                  pl.BlockSpec((tm, 512), lambda i: (i, 1)),
                  row(512), row(D_PLE),
                  const((512, 512)), const((1, 512)), const((1, 512)),
                  const((D_MODEL, D_MODEL)), const((D_PLE, D_MODEL)), const((1, D_MODEL)),
                  const((D_MODEL, D_MODEL)), const((1, D_MODEL))],
        out_specs=row(D_MODEL),
        out_shape=jax.ShapeDtypeStruct((T, D_MODEL), f32),
        compiler_params=pltpu.CompilerParams(
            dimension_semantics=("arbitrary",), vmem_limit_bytes=VMEM_LIMIT),
        name="out_stage",
    )(h, y_f, y_b, bonus, gates, gates, o, p, prm["seg"], prm["ln_g"], prm["ln_b"], prm["w_out"],
      prm["ple_proj"], prm["ple_norm"], prm["gate_w"], prm["gate_b"])


def _chunk_triangles(tm):
    t = np.arange(tm)
    same = (t[:, None] // CHUNK) == (t[None, :] // CHUNK)
    low_incl = same & (t[None, :] <= t[:, None])
    up_strict = same & (t[None, :] > t[:, None])
    up_incl = same & (t[None, :] >= t[:, None])
    low_strict = same & (t[None, :] < t[:, None])
    tri = np.stack([np.concatenate([low_incl, up_strict], 0), np.concatenate([up_incl, low_strict], 0)])
    return jnp.asarray(tri.astype(np.float32), dtype=bf16)


def _layer_params(i, norm_pre, w_in, w_out, rw_conv_w, rw_conv_b, rw_w0, rw_w_up, rw_a0, rw_a_up, rw_k_k,
                  rw_k_a, rw_r_k, rw_ln_g, rw_ln_b, da_q_norm, da_k_norm, da_lambda, da_subln, ple_proj,
                  ple_norm, ple_gate_w, ple_gate_b):
    pad = CONV_PAD - (CONV_COLS)
    w = w_in[i]
    o1 = CONV_COLS + RW_WIDTH
    o4 = o1 + 3 * DA_WIDTH
    w_pad = jnp.concatenate(
        [w[:, :CONV_COLS], jnp.zeros((D_MODEL, pad), f32), w[:, CONV_COLS:o1], w[:, o4:], w[:, o1:o4]],
        axis=1).astype(bf16)
    wup = rw_w_up[i]
    zeros = jnp.zeros((W_LORA, RW_WIDTH), f32)
    w_up_bd = jnp.concatenate(
        [jnp.concatenate([wup[0], zeros], 1), jnp.concatenate([zeros, wup[1]], 1)], 0).astype(bf16)
    a_up_pad = jnp.concatenate([rw_a_up[i], jnp.zeros((64, RW_WIDTH), f32)], 0).astype(bf16)
    lane = np.arange(512)
    seg = jnp.asarray((lane[:, None] // HEAD_DIM == lane[None, :] // HEAD_DIM).astype(np.float32), dtype=bf16)
    return dict(
        norm_pre=norm_pre[i].reshape(1, D_MODEL),
        w_in=w_pad,
        conv_w=jnp.pad(rw_conv_w[i], ((0, 0), (0, pad))),
        conv_b=jnp.pad(rw_conv_b[i], (0, pad)).reshape(1, CONV_PAD),
        w0=rw_w0[i].reshape(1, 2 * RW_WIDTH),
        w_up=w_up_bd,
        a0=rw_a0[i].reshape(1, RW_WIDTH),
        a_up=a_up_pad,
        k_k=rw_k_k[i].reshape(1, RW_WIDTH),
        k_a=rw_k_a[i].reshape(1, RW_WIDTH),
        r_k=rw_r_k[i].reshape(1, RW_WIDTH),
        ln_g=rw_ln_g[i].reshape(1, RW_WIDTH),
        ln_b=rw_ln_b[i].reshape(1, RW_WIDTH),
        q_norm=jnp.tile(da_q_norm[i].reshape(1, 2 * HEAD_DIM), (1, DA_HEADS)),
        k_norm=jnp.tile(da_k_norm[i].reshape(1, 2 * HEAD_DIM), (1, DA_HEADS)),
        reach=_alibi_reach(da_q_norm[i], da_k_norm[i]),
        lam_vec=da_lambda[i],
        subln=da_subln[i].reshape(1, DA_V_DIM),
        w_out=w_out[i].astype(bf16),
        ple_proj=ple_proj[i].astype(bf16),
        ple_norm=ple_norm[i].reshape(1, D_MODEL),
        gate_w=ple_gate_w[i].astype(bf16),
        gate_b=ple_gate_b[i].reshape(1, D_MODEL),
        seg=seg,
        tri=_chunk_triangles(TOKEN_TILE),
    )


def _layer(h, p, lam_init, prm, batch, seq_len):
    gates, q0, q1, k, v, slab, decay, v_rw, bonus = _front(h, seq_len, prm)
    y_f, y_b = _rwkv_scan(slab, decay, v_rw, batch, seq_len)
    o = _flash(q0, q1, k, v, prm["reach"], prm["lam_vec"], prm["subln"], lam_init, batch, seq_len)
    return _out_stage(h, y_f, y_b, bonus, gates, o, p, prm)


def _trunk(x, p, layers):
    batch, seq_len, _ = x.shape
    h = x.reshape(batch * seq_len, D_MODEL)
    for i, prm in enumerate(layers):
        lam_init = 0.8 - 0.6 * math.exp(-0.3 * i)
        h = _layer(h, p[i].reshape(batch * seq_len, D_PLE), lam_init, prm, batch, seq_len)
    return h.reshape(batch, seq_len, D_MODEL)


def kernel(x_prompt, x_sample, p_prompt, p_sample, norm_pre, w_in, w_out, rw_conv_w, rw_conv_b, rw_w0, rw_w_up, rw_a0, rw_a_up, rw_k_k, rw_k_a, rw_r_k, rw_ln_g, rw_ln_b, da_q_norm, da_k_norm, da_lambda, da_subln, ple_proj, ple_norm, ple_gate_w, ple_gate_b):
    depth = norm_pre.shape[0]
    layers = [_layer_params(i, norm_pre, w_in, w_out, rw_conv_w, rw_conv_b, rw_w0, rw_w_up, rw_a0, rw_a_up,
                            rw_k_k, rw_k_a, rw_r_k, rw_ln_g, rw_ln_b, da_q_norm, da_k_norm, da_lambda,
                            da_subln, ple_proj, ple_norm, ple_gate_w, ple_gate_b) for i in range(depth)]
    return (_trunk(x_prompt, p_prompt, layers), _trunk(x_sample, p_sample, layers))
```
